```python
import math
import jax, jax.numpy as jnp
from jax import lax
import numpy as np

D_MODEL = 1024
BATCH = 4
SEQ = 8192
DEPTH = 2

HEAD_DIM = 64
N_META = 16
BLOCK_Q = 128
N_PAD = BLOCK_Q - N_META
NEG = -1e30
SB_HEADS = (D_MODEL // 2) // HEAD_DIM
SB_W = SB_HEADS * HEAD_DIM
DIFF_HEADS = (D_MODEL // 2) // (2 * HEAD_DIM)
DIFF_W = DIFF_HEADS * 2 * HEAD_DIM
EVEN_IN = 3 * SB_W + 3 * DIFF_W
FOX_HEADS = (D_MODEL // 2) // HEAD_DIM
FOX_W = FOX_HEADS * HEAD_DIM
S5_CHANNELS = D_MODEL // 2
S5_GROUP = 16
S5_GROUPS = S5_CHANNELS // S5_GROUP
S5_STATE = 64
ODD_IN = 3 * FOX_W + FOX_HEADS + S5_CHANNELS
D_FF = -(-8 * D_MODEL // (3 * 256)) * 256
N_EVEN = (DEPTH + 1) // 2
N_ODD = DEPTH // 2

kernel_name = "hybrid_stickbreak_diff_fox_s5_block"


def rms_norm(x, g, eps=1e-6):
    xf = x.astype(jnp.float32)
    y = xf * lax.rsqrt(jnp.mean(xf * xf, axis=-1, keepdims=True) + eps)
    return (y * g.astype(jnp.float32)).astype(x.dtype)


def sweep_blocks(block_fn, n_pos):
    starts = jnp.arange(n_pos // BLOCK_Q, dtype=jnp.int32) * BLOCK_Q
    out = lax.map(block_fn, starts)
    nb, b, bq, h, dv = out.shape
    return jnp.transpose(out, (1, 0, 2, 3, 4)).reshape(b, nb * bq, h, dv)


def stick_breaking_attention(q, k, v):
    L = q.shape[1]
    scale = HEAD_DIM ** -0.5
    kpos = jnp.arange(L)

    def block(start):
        qb = lax.dynamic_slice_in_dim(q, start, BLOCK_Q, axis=1)
        qpos = start + jnp.arange(BLOCK_Q)
        logits = jnp.einsum('bqhd,bkhd->bhqk', qb, k).astype(jnp.float32) * scale
        mask = (kpos[None, :] < qpos[:, None]) & (kpos[None, :] >= N_PAD)
        log_beta = jax.nn.log_sigmoid(logits)
        log_1m = jnp.where(mask, jax.nn.log_sigmoid(-logits), 0.0)
        log_w = log_beta + lax.cumsum(log_1m, axis=3, reverse=True) - log_1m
        w = jnp.where(mask, jnp.exp(log_w), 0.0)
        return jnp.einsum('bhqk,bkhd->bqhd', w.astype(v.dtype), v)

    return sweep_blocks(block, L)


def differential_attention(q, k, v, lam):
    L = q.shape[1]
    scale = HEAD_DIM ** -0.5
    kpos = jnp.arange(L)

    def block(start):
        qb = lax.dynamic_slice_in_dim(q, start, BLOCK_Q, axis=1)
        qpos = start + jnp.arange(BLOCK_Q)
        logits = jnp.einsum('bqhcd,bkhcd->bhcqk', qb, k).astype(jnp.float32) * scale
        mask = (kpos[None, :] <= qpos[:, None]) & (kpos[None, :] >= N_PAD)
        p = jax.nn.softmax(jnp.where(mask, logits, NEG), axis=-1)
        w = p[:, :, 0] - lam * p[:, :, 1]
        return jnp.einsum('bhqk,bkhe->bqhe', w.astype(v.dtype), v)

    return sweep_blocks(block, L)


def forgetting_attention(q, k, v, log_f):
    L = q.shape[1]
    scale = HEAD_DIM ** -0.5
    kpos = jnp.arange(L)
    f_cum = jnp.transpose(jnp.cumsum(log_f, axis=1), (0, 2, 1))

    def block(start):
        qb = lax.dynamic_slice_in_dim(q, start, BLOCK_Q, axis=1)
        fq = lax.dynamic_slice_in_dim(f_cum, start, BLOCK_Q, axis=2)
        qpos = start + jnp.arange(BLOCK_Q)
        logits = (jnp.einsum('bqhd,bkhd->bhqk', qb, k).astype(jnp.float32) * scale
                  + fq[..., :, None] - f_cum[..., None, :])
        mask = (kpos[None, :] <= qpos[:, None]) & (kpos[None, :] >= N_PAD)
        p = jax.nn.softmax(jnp.where(mask, logits, NEG), axis=-1)
        return jnp.einsum('bhqk,bkhd->bqhd', p.astype(v.dtype), v)

    return sweep_blocks(block, L)


def s5_glu(u, lam_re, lam_im, log_dt, b_re, b_im, c_re, c_im, d_skip, w_glu, b_glu):
    bsz, L, _ = u.shape
    f32 = jnp.float32
    uf = u.astype(f32).reshape(bsz, L, S5_GROUPS, S5_GROUP)
    lre, lim = lam_re.astype(f32), lam_im.astype(f32)
    dt = jnp.exp(log_dt.astype(f32))[:, None]
    mag = jnp.exp(lre * dt)
    a_re, a_im = mag * jnp.cos(lim * dt), mag * jnp.sin(lim * dt)
    den = lre * lre + lim * lim
    g_re = ((a_re - 1.0) * lre + a_im * lim) / den
    g_im = (a_im * lre - (a_re - 1.0) * lim) / den
    br, bi = b_re.astype(f32), b_im.astype(f32)
    bb_re = g_re[..., None] * br - g_im[..., None] * bi
    bb_im = g_re[..., None] * bi + g_im[..., None] * br
    bu_re = jnp.einsum('blgh,gph->blgp', uf, bb_re)
    bu_im = jnp.einsum('blgh,gph->blgp', uf, bb_im)
    shape_a = (1, L, S5_GROUPS, S5_STATE)
    a_re_l = jnp.broadcast_to(a_re[None, None], shape_a)
    a_im_l = jnp.broadcast_to(a_im[None, None], shape_a)

    def combine(e1, e2):
        a1r, a1i, b1r, b1i = e1
        a2r, a2i, b2r, b2i = e2
        return (a2r * a1r - a2i * a1i, a2r * a1i + a2i * a1r,
                a2r * b1r - a2i * b1i + b2r, a2r * b1i + a2i * b1r + b2i)

    _, _, x_re, x_im = lax.associative_scan(combine, (a_re_l, a_im_l, bu_re, bu_im), axis=1)
    y = (jnp.einsum('blgp,ghp->blgh', x_re, c_re.astype(f32))
         - jnp.einsum('blgp,ghp->blgh', x_im, c_im.astype(f32))
         + d_skip.astype(f32) * uf).reshape(bsz, L, S5_CHANNELS)
    y = jax.nn.gelu(y)
    out = y * jax.nn.sigmoid(y @ w_glu.astype(f32) + b_glu.astype(f32))
    return out.astype(u.dtype)


def even_mixer(h, w_in, w_out, lam_q1, lam_k1, lam_q2, lam_k2, subln_g, lam_init):
    bsz, L, _ = h.shape
    proj = h @ w_in
    sb_q, sb_k, sb_v, df_q, df_k, df_v = jnp.split(
        proj, [SB_W, 2 * SB_W, 3 * SB_W, 3 * SB_W + DIFF_W, 3 * SB_W + 2 * DIFF_W], axis=-1)
    hd = (bsz, L, SB_HEADS, HEAD_DIM)
    sb = stick_breaking_attention(sb_q.reshape(hd), sb_k.reshape(hd), sb_v.reshape(hd))
    f32 = jnp.float32
    lam = (jnp.exp(jnp.sum(lam_q1.astype(f32) * lam_k1.astype(f32)))
           - jnp.exp(jnp.sum(lam_q2.astype(f32) * lam_k2.astype(f32))) + lam_init)
    dqk = (bsz, L, DIFF_HEADS, 2, HEAD_DIM)
    df = differential_attention(df_q.reshape(dqk), df_k.reshape(dqk),
                                df_v.reshape(bsz, L, DIFF_HEADS, 2 * HEAD_DIM), lam)
    df = rms_norm(df, subln_g) * (1.0 - lam_init)
    cat = jnp.concatenate([sb.reshape(bsz, L, SB_W), df.reshape(bsz, L, DIFF_W).astype(h.dtype)], axis=-1)
    return cat @ w_out


def odd_mixer(h, w_in, w_out, b_f, lam_re, lam_im, log_dt, b_re, b_im, c_re, c_im, d_skip, w_glu, b_glu):
    bsz, L, _ = h.shape
    proj = h @ w_in
    q, k, v, f_logit, u = jnp.split(
        proj, [FOX_W, 2 * FOX_W, 3 * FOX_W, 3 * FOX_W + FOX_HEADS], axis=-1)
    hd = (bsz, L, FOX_HEADS, HEAD_DIM)
    log_f = jax.nn.log_sigmoid(f_logit.astype(jnp.float32) + b_f.astype(jnp.float32))
    fox = forgetting_attention(q.reshape(hd), k.reshape(hd), v.reshape(hd), log_f)
    ssm = s5_glu(u, lam_re, lam_im, log_dt, b_re, b_im, c_re, c_im, d_skip, w_glu, b_glu)
    cat = jnp.concatenate([fox.reshape(bsz, L, FOX_W), ssm], axis=-1)
    return cat @ w_out


def swiglu(h, w_gate_up, w_down):
    g, u = jnp.split(h @ w_gate_up, 2, axis=-1)
    return (jax.nn.silu(g) * u) @ w_down


def setup_inputs(seed: int = 0) -> dict:
    key = jax.random.key(seed)
    ks = jax.random.split(key, 32)
    f32 = jnp.float32
    nrm = lambda k, s, sc: jax.random.normal(k, s, f32) * sc
    n_idx = jnp.arange(S5_STATE, dtype=f32)
    return {
        "x": nrm(ks[0], (BATCH, SEQ, D_MODEL), 1.0),
        "meta_tokens": nrm(ks[1], (N_META, D_MODEL), 1.0),
        "norm_mix_g": 1.0 + nrm(ks[2], (DEPTH, D_MODEL), 0.01),
        "norm_ffn_g": 1.0 + nrm(ks[3], (DEPTH, D_MODEL), 0.01),
        "final_norm_g": 1.0 + nrm(ks[4], (D_MODEL,), 0.01),
        "even_w_in": nrm(ks[5], (N_EVEN, D_MODEL, EVEN_IN), D_MODEL ** -0.5),
        "even_w_out": nrm(ks[6], (N_EVEN, SB_W + DIFF_W, D_MODEL), (SB_W + DIFF_W) ** -0.5),
        "diff_lam_q1": nrm(ks[7], (N_EVEN, HEAD_DIM), 0.1),
        "diff_lam_k1": nrm(ks[8], (N_EVEN, HEAD_DIM), 0.1),
        "diff_lam_q2": nrm(ks[9], (N_EVEN, HEAD_DIM), 0.1),
        "diff_lam_k2": nrm(ks[10], (N_EVEN, HEAD_DIM), 0.1),
        "diff_subln_g": 1.0 + nrm(ks[11], (N_EVEN, 2 * HEAD_DIM), 0.01),
        "odd_w_in": nrm(ks[12], (N_ODD, D_MODEL, ODD_IN), D_MODEL ** -0.5),
        "odd_w_out": nrm(ks[13], (N_ODD, FOX_W + S5_CHANNELS, D_MODEL), (FOX_W + S5_CHANNELS) ** -0.5),
        "fox_b_f": jax.random.uniform(ks[14], (N_ODD, FOX_HEADS), f32, 1.0, 4.0),
        "s5_lam_re": -0.5 + nrm(ks[15], (N_ODD, S5_GROUPS, S5_STATE), 0.01),
        "s5_lam_im": math.pi * n_idx + nrm(ks[16], (N_ODD, S5_GROUPS, S5_STATE), 0.01),
        "s5_log_dt": jax.random.uniform(ks[17], (N_ODD, S5_GROUPS), f32, math.log(1e-3), math.log(1e-1)),
        "s5_b_re": nrm(ks[18], (N_ODD, S5_GROUPS, S5_STATE, S5_GROUP), (2 * S5_GROUP) ** -0.5),
        "s5_b_im": nrm(ks[19], (N_ODD, S5_GROUPS, S5_STATE, S5_GROUP), (2 * S5_GROUP) ** -0.5),
        "s5_c_re": nrm(ks[20], (N_ODD, S5_GROUPS, S5_GROUP, S5_STATE), S5_STATE ** -0.5),
        "s5_c_im": nrm(ks[21], (N_ODD, S5_GROUPS, S5_GROUP, S5_STATE), S5_STATE ** -0.5),
        "s5_d": nrm(ks[22], (N_ODD, S5_GROUPS, S5_GROUP), 1.0),
        "s5_w_glu": nrm(ks[23], (N_ODD, S5_CHANNELS, S5_CHANNELS), S5_CHANNELS ** -0.5),
        "s5_b_glu": nrm(ks[24], (N_ODD, S5_CHANNELS), 0.01),
        "ffn_w_gate_up": nrm(ks[25], (DEPTH, D_MODEL, 2 * D_FF), D_MODEL ** -0.5),
        "ffn_w_down": nrm(ks[26], (DEPTH, D_FF, D_MODEL), D_FF ** -0.5),
    }


def reference(x, meta_tokens, norm_mix_g, norm_ffn_g, final_norm_g,
              even_w_in, even_w_out, diff_lam_q1, diff_lam_k1, diff_lam_q2, diff_lam_k2, diff_subln_g,
              odd_w_in, odd_w_out, fox_b_f, s5_lam_re, s5_lam_im, s5_log_dt, s5_b_re, s5_b_im,
              s5_c_re, s5_c_im, s5_d, s5_w_glu, s5_b_glu, ffn_w_gate_up, ffn_w_down):
    bsz = x.shape[0]
    pad = jnp.zeros((bsz, N_PAD, D_MODEL), x.dtype)
    meta = jnp.broadcast_to(meta_tokens[None].astype(x.dtype), (bsz, N_META, D_MODEL))
    h = jnp.concatenate([pad, meta, x], axis=1)
    L = h.shape[1]
    valid = (jnp.arange(L) >= N_PAD).astype(x.dtype)[None, :, None]
    for i in range(DEPTH):
        j = i // 2
        hn = rms_norm(h, norm_mix_g[i])
        if i % 2 == 0:
            lam_init = 0.8 - 0.6 * math.exp(-0.3 * i)
            mix = even_mixer(hn, even_w_in[j], even_w_out[j], diff_lam_q1[j], diff_lam_k1[j],
                             diff_lam_q2[j], diff_lam_k2[j], diff_subln_g[j], lam_init)
        else:
            mix = odd_mixer(hn, odd_w_in[j], odd_w_out[j], fox_b_f[j], s5_lam_re[j], s5_lam_im[j],
                            s5_log_dt[j], s5_b_re[j], s5_b_im[j], s5_c_re[j], s5_c_im[j], s5_d[j],
                            s5_w_glu[j], s5_b_glu[j])
        h = h + mix * valid
        h = h + swiglu(rms_norm(h, norm_ffn_g[i]), ffn_w_gate_up[i], ffn_w_down[i]) * valid
    h = rms_norm(h, final_norm_g)
    return h[:, BLOCK_Q:]
```

```python
import functools
import math

import jax
import jax.numpy as jnp
from jax import lax
from jax.experimental import pallas as pl
from jax.experimental.pallas import tpu as pltpu

F32 = jnp.float32
BF16 = jnp.bfloat16

HEAD_DIM = 64
LANES = 128
N_META = 16
ATT_TILE = 256
FRONT = ATT_TILE
N_PAD = FRONT - N_META
NEG = -1e30
RMS_EPS = 1e-6
S5_GROUP = 16
S5_STATE = 64
S5_T = 16
VMEM_LIMIT = 56 * 1024 * 1024


def _cparams(n_axes):
    return pltpu.CompilerParams(
        dimension_semantics=("arbitrary",) * n_axes, vmem_limit_bytes=VMEM_LIMIT)


def _resident(shape, index_map):
    return pl.BlockSpec(shape, index_map, pipeline_mode=pl.Buffered(1))


def _split_bf16(w):
    hi = w.astype(BF16)
    lo = (w - hi.astype(F32)).astype(BF16)
    return hi, lo


def _dot(a, b):
    return jnp.dot(a, b, preferred_element_type=F32)


def _dot_nt(a, b):
    return lax.dot_general(a, b, (((1,), (1,)), ((), ())), preferred_element_type=F32)


def _dot3(a_hi, a_lo, b_hi, b_lo):
    return _dot(a_hi, b_hi) + _dot(a_lo, b_hi) + _dot(a_hi, b_lo)


def _rms(x, g):
    ms = jnp.mean(x * x, axis=-1, keepdims=True)
    return x * lax.rsqrt(ms + RMS_EPS) * g


def _proj_even_kernel(x_ref, g_ref, w_ref, o_ref, *, n_chunk):
    hn = _rms(x_ref[...], g_ref[...]).astype(BF16)
    n = o_ref.shape[-1]
    for c in range(0, n, n_chunk):
        o_ref[:, c:c + n_chunk] = _dot(hn, w_ref[:, c:c + n_chunk]).astype(o_ref.dtype)


def _proj_odd_kernel(x_ref, g_ref, wqkv_ref, wu_ref, wfh_ref, wfl_ref,
                     qkv_ref, u_ref, ft_ref, *, n_chunk):
    hn32 = _rms(x_ref[...], g_ref[...])
    hn = hn32.astype(BF16)
    hn_lo = (hn32 - hn.astype(F32)).astype(BF16)
    n = qkv_ref.shape[-1]
    for c in range(0, n, n_chunk):
        qkv_ref[:, c:c + n_chunk] = _dot(hn, wqkv_ref[:, c:c + n_chunk]).astype(qkv_ref.dtype)
    u_ref[...] = _dot(hn, wu_ref[...])
    wfh = wfh_ref[...]
    ft_ref[...] = _dot_nt(wfh, hn) + _dot_nt(wfh, hn_lo) + _dot_nt(wfl_ref[...], hn)


def _norm_proj_even(h, g, w):
    m, d = h.shape
    tm = next(t for t in (512, 256) if m % t == 0)
    n = w.shape[1]
    return pl.pallas_call(
        functools.partial(_proj_even_kernel, n_chunk=512),
        out_shape=jax.ShapeDtypeStruct((m, n), BF16),
        grid=(m // tm,),
        in_specs=[pl.BlockSpec((tm, d), lambda i: (i, 0)),
                  _resident((1, d), lambda i: (0, 0)),
                  _resident((d, n), lambda i: (0, 0))],
        out_specs=pl.BlockSpec((tm, n), lambda i: (i, 0)),
        compiler_params=_cparams(1),
        name="norm_proj_even",
    )(h, g, w)


def _norm_proj_odd(h, g, wqkv, wu, wfh, wfl):
    m, d = h.shape
    tm = next(t for t in (512, 256) if m % t == 0)
    n = wqkv.shape[1]
    nu = wu.shape[1]
    nf = wfh.shape[0]
    return pl.pallas_call(
        functools.partial(_proj_odd_kernel, n_chunk=512),
        out_shape=(jax.ShapeDtypeStruct((m, n), BF16),
                   jax.ShapeDtypeStruct((m, nu), F32),
                   jax.ShapeDtypeStruct((nf, m), F32)),
        grid=(m // tm,),
        in_specs=[pl.BlockSpec((tm, d), lambda i: (i, 0)),
                  _resident((1, d), lambda i: (0, 0)),
                  _resident((d, n), lambda i: (0, 0)),
                  _resident((d, nu), lambda i: (0, 0)),
                  _resident((nf, d), lambda i: (0, 0)),
                  _resident((nf, d), lambda i: (0, 0))],
        out_specs=(pl.BlockSpec((tm, n), lambda i: (i, 0)),
                   pl.BlockSpec((tm, nu), lambda i: (i, 0)),
                   pl.BlockSpec((nf, tm), lambda i: (0, i))),
        compiler_params=_cparams(1),
        name="norm_proj_odd",
    )(h, g, wqkv, wu, wfh, wfl)


def _sb_kernel(q_ref, k_ref, v_ref, tri_ref, o_ref, acc_ref, carry_ref, *, tile, n_pad):
    i = pl.program_id(2)
    q = q_ref[...]
    lane = lax.broadcasted_iota(jnp.int32, (1, LANES), 1)
    head_lanes = (lane < HEAD_DIM, lane >= HEAD_DIM)
    acc_ref[...] = jnp.zeros_like(acc_ref)
    carry_ref[...] = jnp.zeros_like(carry_ref)

    def step(j, mode):
        start = pl.multiple_of(j * tile, tile)
        ks = k_ref[pl.ds(start, tile), :]
        vs = v_ref[pl.ds(start, tile), :]
        if mode == "diag":
            row = lax.broadcasted_iota(jnp.int32, (tile, tile), 0)
            col = lax.broadcasted_iota(jnp.int32, (tile, tile), 1)
            mask = (col < row) & (col + start >= n_pad)
        elif mode == "first":
            mask = lax.broadcasted_iota(jnp.int32, (1, tile), 1) >= n_pad
        else:
            mask = None
        total = acc_ref[...]
        for h in range(2):
            kh = jnp.where(head_lanes[h], ks, jnp.zeros_like(ks))
            vh = jnp.where(head_lanes[h], vs, jnp.zeros_like(vs))
            logit = _dot_nt(q, kh)
            sp = jnp.maximum(logit, 0.0) + jnp.log(1.0 + jnp.exp(-jnp.abs(logit)))
            log_beta = logit - sp
            log_1m = -sp
            if mask is not None:
                log_1m = jnp.where(mask, log_1m, 0.0)
            hi = log_1m.astype(BF16)
            lo = (log_1m - hi.astype(F32)).astype(BF16)
            suffix = _dot(jnp.concatenate([hi, lo], axis=1), tri_ref[...])
            carry = carry_ref[h]
            w = jnp.exp(log_beta + suffix + carry)
            if mask is not None:
                w = jnp.where(mask, w, 0.0)
            carry_ref[h] = carry + jnp.sum(log_1m, axis=1, keepdims=True)
            total = total + _dot(w.astype(BF16), vh)
        acc_ref[...] = total

    step(i, "diag")

    def mid(n, c):
        step(i - n, "mid")
        return c

    lax.fori_loop(1, i, mid, 0)

    @pl.when(i > 0)
    def _():
        step(0, "first")

    o_ref[...] = acc_ref[...].astype(o_ref.dtype)


def _sb_attention(proj, bsz, seq, q_blk, k_blk, v_blk, n_blk):
    tile = ATT_TILE
    idx = jnp.arange(tile)
    tri = (idx[:, None] > idx[None, :]).astype(BF16)
    tri2 = jnp.concatenate([tri, tri], axis=0)
    return pl.pallas_call(
        functools.partial(_sb_kernel, tile=tile, n_pad=N_PAD),
        out_shape=jax.ShapeDtypeStruct((bsz, seq, n_blk * LANES), BF16),
        grid=(bsz, n_blk, seq // tile),
        in_specs=[pl.BlockSpec((None, tile, LANES), lambda b, p, i: (b, i, q_blk + p)),
                  pl.BlockSpec((None, seq, LANES), lambda b, p, i: (b, 0, k_blk + p)),
                  pl.BlockSpec((None, seq, LANES), lambda b, p, i: (b, 0, v_blk + p)),
                  _resident((2 * tile, tile), lambda b, p, i: (0, 0))],
        out_specs=pl.BlockSpec((None, tile, LANES), lambda b, p, i: (b, i, p)),
        scratch_shapes=[pltpu.VMEM((tile, LANES), F32),
                        pltpu.VMEM((2, tile, 1), F32)],
        compiler_params=_cparams(3),
        name="stick_breaking_attention",
    )(proj, proj, proj, tri2)


def _softmax_attn_kernel(*refs, tile, n_pad, mode, out_scale):
    if mode == "fox":
        q_ref, k_ref, v_ref, fcol_ref, frow_ref, o_ref, acc_ref, m_ref, l_ref = refs
    else:
        q_ref, k_ref, v_ref, lam_ref, g_ref, o_ref, acc_ref, m_ref, l_ref = refs
    i = pl.program_id(2)
    p_blk = pl.program_id(1)
    q = q_ref[...]
    lane = lax.broadcasted_iota(jnp.int32, (1, LANES), 1)
    head_lanes = (lane < HEAD_DIM, lane >= HEAD_DIM)
    acc_ref[...] = jnp.zeros_like(acc_ref)
    l_ref[...] = jnp.zeros_like(l_ref)
    m_ref[...] = jnp.full_like(m_ref, NEG)

    def step(j, kind):
        start = pl.multiple_of(j * tile, tile)
        ks = k_ref[pl.ds(start, tile), :]
        vs = v_ref[pl.ds(start, tile), :]
        if kind == "diag":
            row = lax.broadcasted_iota(jnp.int32, (tile, tile), 0)
            col = lax.broadcasted_iota(jnp.int32, (tile, tile), 1)
            mask = (col <= row) & (col + start >= n_pad)
        elif kind == "first":
            mask = lax.broadcasted_iota(jnp.int32, (1, tile), 1) >= n_pad
        else:
            mask = None
        for h in range(2):
            kh = jnp.where(head_lanes[h], ks, jnp.zeros_like(ks))
            s = _dot_nt(q, kh)
            if mode == "fox":
                fq = fcol_ref[:, h:h + 1]
                fk = frow_ref[h:h + 1, pl.ds(start, tile)]
                s = s + fq - fk
                vh = jnp.where(head_lanes[h], vs, jnp.zeros_like(vs))
            else:
                vh = vs
            if mask is not None:
                s = jnp.where(mask, s, NEG)
            m_old = m_ref[h]
            m_new = jnp.maximum(m_old, jnp.max(s, axis=1, keepdims=True))
            alpha = jnp.exp(m_old - m_new)
            p = jnp.exp(s - m_new)
            m_ref[h] = m_new
            l_ref[h] = alpha * l_ref[h] + (p[:, :LANES] + p[:, LANES:])
            acc_ref[h] = alpha * acc_ref[h] + _dot(p.astype(BF16), vh)

    @pl.when(i > 0)
    def _():
        step(0, "first")

    def mid(j, c):
        step(j, "mid")
        return c

    lax.fori_loop(1, i, mid, 0)
    step(i, "diag")

    inv = [1.0 / jnp.sum(l_ref[h], axis=1, keepdims=True) for h in range(2)]
    if mode == "fox":
        out = jnp.where(head_lanes[0], acc_ref[0] * inv[0], acc_ref[1] * inv[1])
    else:
        out = acc_ref[0] * inv[0] - lam_ref[0, 0] * (acc_ref[1] * inv[1])
        out = _rms(out, g_ref[...]) * out_scale
    o_ref[...] = out.astype(o_ref.dtype)


def _softmax_attention(mode, proj, bsz, seq, q_blk, k_blk, v_blk, n_blk, extra, out_scale=1.0):
    tile = ATT_TILE
    in_specs = [pl.BlockSpec((None, tile, LANES), lambda b, p, i: (b, i, q_blk + p)),
                pl.BlockSpec((None, seq, LANES), lambda b, p, i: (b, 0, k_blk + p)),
                pl.BlockSpec((None, seq, LANES), lambda b, p, i: (b, 0, v_blk + p))]
    if mode == "fox":
        f_col, f_row = extra
        in_specs += [pl.BlockSpec((None, None, tile, 2), lambda b, p, i: (b, p, i, 0)),
                     pl.BlockSpec((None, None, 2, seq), lambda b, p, i: (b, p, 0, 0))]
    else:
        in_specs += [pl.BlockSpec(memory_space=pltpu.SMEM),
                     _resident((1, LANES), lambda b, p, i: (0, 0))]
    return pl.pallas_call(
        functools.partial(_softmax_attn_kernel, tile=tile, n_pad=N_PAD, mode=mode,
                          out_scale=out_scale),
        out_shape=jax.ShapeDtypeStruct((bsz, seq, n_blk * LANES), BF16),
        grid=(bsz, n_blk, seq // tile),
        in_specs=in_specs,
        out_specs=pl.BlockSpec((None, tile, LANES), lambda b, p, i: (b, i, p)),
        scratch_shapes=[pltpu.VMEM((2, tile, LANES), F32),
                        pltpu.VMEM((2, tile, 1), F32),
                        pltpu.VMEM((2, tile, LANES), F32)],
        compiler_params=_cparams(3),
        name=mode + "_attention",
    )(proj, proj, proj, *extra)


def _fcum_kernel(ft_ref, b_ref, tri_ref, o_ref, *, tile, n_pad):
    seq = ft_ref.shape[-1]
    tri = tri_ref[...]
    carry = jnp.zeros((ft_ref.shape[0], 1), F32)
    for c in range(seq // tile):
        x = ft_ref[:, c * tile:(c + 1) * tile] + b_ref[...]
        log_f = jnp.minimum(x, 0.0) - jnp.log(1.0 + jnp.exp(-jnp.abs(x)))
        pos = lax.broadcasted_iota(jnp.int32, (1, tile), 1) + c * tile
        log_f = jnp.where(pos >= n_pad, log_f, 0.0)
        hi = log_f.astype(BF16)
        r1 = log_f - hi.astype(F32)
        mid = r1.astype(BF16)
        lo = (r1 - mid.astype(F32)).astype(BF16)
        cs = _dot(jnp.concatenate([hi, mid, lo], axis=1), tri)
        o_ref[:, c * tile:(c + 1) * tile] = cs + carry
        carry = carry + jnp.sum(log_f, axis=1, keepdims=True)


def _forget_cumsum(ft, b_f):
    bsz, rows, seq = ft.shape
    tile = ATT_TILE
    idx = jnp.arange(tile)
    tri = (idx[:, None] <= idx[None, :]).astype(BF16)
    tri3 = jnp.concatenate([tri, tri, tri], axis=0)
    return pl.pallas_call(
        functools.partial(_fcum_kernel, tile=tile, n_pad=N_PAD),
        out_shape=jax.ShapeDtypeStruct((bsz, rows, seq), F32),
        grid=(bsz,),
        in_specs=[pl.BlockSpec((None, rows, seq), lambda b: (b, 0, 0)),
                  _resident((rows, 1), lambda b: (0, 0)),
                  _resident((3 * tile, tile), lambda b: (0, 0))],
        out_specs=pl.BlockSpec((None, rows, seq), lambda b: (b, 0, 0)),
        compiler_params=_cparams(1),
        name="forget_cumsum",
    )(ft, b_f, tri3)


def _s5_kernel(u_ref, toep_h, toep_l, sbr_h, sbr_l, sbi_h, sbi_l, at_ref,
               car_h, car_l, cai_h, cai_l, y_ref, sr_ref, si_ref, xr_ref, xi_ref,
               *, bsz, row_chunk):
    rows = u_ref.shape[0]
    n_chunks = rows // bsz

    def split(x):
        hi = x.astype(BF16)
        return hi, (x - hi.astype(F32)).astype(BF16)

    def stage1(r, c):
        sl = pl.ds(pl.multiple_of(r * row_chunk, 8), row_chunk)
        uh, ul = split(u_ref[sl, :])
        y_ref[sl, :] = _dot3(uh, ul, toep_h[...], toep_l[...])
        sr_ref[sl, :] = _dot3(uh, ul, sbr_h[...], sbr_l[...])
        si_ref[sl, :] = _dot3(uh, ul, sbi_h[...], sbi_l[...])
        return c

    lax.fori_loop(0, rows // row_chunk, stage1, 0)

    a_re = at_ref[0:1, :]
    a_im = at_ref[1:2, :]

    def scan(c, state):
        x_re, x_im = state
        sl = pl.ds(c * bsz, bsz)
        xr_ref[sl, :] = x_re
        xi_ref[sl, :] = x_im
        n_re = a_re * x_re - a_im * x_im + sr_ref[sl, :]
        n_im = a_re * x_im + a_im * x_re + si_ref[sl, :]
        return n_re, n_im

    zero = jnp.zeros((bsz, LANES), F32)
    lax.fori_loop(0, n_chunks, scan, (zero, zero))

    def stage3(r, c):
        sl = pl.ds(pl.multiple_of(r * row_chunk, 8), row_chunk)
        xrh, xrl = split(xr_ref[sl, :])
        xih, xil = split(xi_ref[sl, :])
        y = (y_ref[sl, :] + _dot3(xrh, xrl, car_h[...], car_l[...])
             + _dot3(xih, xil, cai_h[...], cai_l[...]))
        y_ref[sl, :] = 0.5 * y * (1.0 + jnp.tanh(
            math.sqrt(2.0 / math.pi) * (y + 0.044715 * (y * y * y))))
        return c

    lax.fori_loop(0, rows // row_chunk, stage3, 0)


def _s5_operators(lam_re, lam_im, log_dt, b_re, b_im, c_re, c_im, d_skip):
    t = S5_T
    lre, lim = lam_re.astype(F32), lam_im.astype(F32)
    dt = jnp.exp(log_dt.astype(F32))[:, None]
    mag = jnp.exp(lre * dt)
    a_re, a_im = mag * jnp.cos(lim * dt), mag * jnp.sin(lim * dt)
    den = lre * lre + lim * lim
    g_re = ((a_re - 1.0) * lre + a_im * lim) / den
    g_im = (a_im * lre - (a_re - 1.0) * lim) / den
    br, bi = b_re.astype(F32), b_im.astype(F32)
    bb_re = g_re[..., None] * br - g_im[..., None] * bi
    bb_im = g_re[..., None] * bi + g_im[..., None] * br
    k = jnp.arange(t + 1, dtype=F32)[:, None, None]
    pw_mag = jnp.exp(k * (lre * dt)[None])
    pw_re = pw_mag * jnp.cos(k * (lim * dt)[None])
    pw_im = pw_mag * jnp.sin(k * (lim * dt)[None])
    cr, ci = c_re.astype(F32), c_im.astype(F32)
    hp = lax.Precision.HIGHEST
    ab_re = pw_re[..., None] * bb_re[None] - pw_im[..., None] * bb_im[None]
    ab_im = pw_re[..., None] * bb_im[None] + pw_im[..., None] * bb_re[None]
    kern = (jnp.einsum('ghp,kgpe->kghe', cr, ab_re, precision=hp)
            - jnp.einsum('ghp,kgpe->kghe', ci, ab_im, precision=hp))
    lag = jnp.arange(t)[None, :] - jnp.arange(t)[:, None]
    toep = jnp.where((lag >= 0)[:, :, None, None, None],
                     kern[jnp.clip(lag, 0, t)], 0.0)
    eye_t = jnp.eye(t, dtype=F32)
    eye_h = jnp.eye(S5_GROUP, dtype=F32)
    toep = toep + (eye_t[:, :, None, None, None] * eye_h[None, None, None]
                   * d_skip.astype(F32)[None, None, :, :, None])
    toep = jnp.transpose(toep, (2, 0, 4, 1, 3))
    g_n = toep.shape[0]
    toep = toep.reshape(g_n, t * S5_GROUP, t * S5_GROUP)
    rev = t - 1 - jnp.arange(t)
    sb_re = jnp.transpose(ab_re[rev], (1, 0, 3, 2)).reshape(g_n, t * S5_GROUP, S5_STATE)
    sb_im = jnp.transpose(ab_im[rev], (1, 0, 3, 2)).reshape(g_n, t * S5_GROUP, S5_STATE)
    e_re, e_im = pw_re[1:], pw_im[1:]
    ca_re = cr[None] * e_re[:, :, None, :] - ci[None] * e_im[:, :, None, :]
    ca_im = -(cr[None] * e_im[:, :, None, :] + ci[None] * e_re[:, :, None, :])
    ca_re = jnp.transpose(ca_re, (1, 3, 0, 2)).reshape(g_n, S5_STATE, t * S5_GROUP)
    ca_im = jnp.transpose(ca_im, (1, 3, 0, 2)).reshape(g_n, S5_STATE, t * S5_GROUP)
    at = jnp.stack([pw_re[t], pw_im[t]], axis=1)

    def pair_diag(x):
        g2 = x.reshape(g_n // 2, 2, *x.shape[1:])
        z = jnp.zeros_like(g2[:, 0])
        return jnp.concatenate([jnp.concatenate([g2[:, 0], z], axis=2),
                                jnp.concatenate([z, g2[:, 1]], axis=2)], axis=1)

    at_pair = at.reshape(g_n // 2, 2, 2, S5_STATE).transpose(0, 2, 1, 3).reshape(
        g_n // 2, 2, 2 * S5_STATE)
    ops = [pair_diag(toep), pair_diag(sb_re), pair_diag(sb_im)]
    ops2 = [pair_diag(ca_re), pair_diag(ca_im)]
    out = []
    for x in ops:
        out += list(_split_bf16(x))
    out.append(at_pair)
    for x in ops2:
        out += list(_split_bf16(x))
    return out


def _s5_layer(u, ops):
    bsz, seq, ch = u.shape
    t = S5_T
    n_pairs = ch // (2 * S5_GROUP)
    n_chunks = seq // t
    rows = n_chunks * bsz
    width = 2 * t * S5_GROUP
    ut = u.reshape(bsz, n_chunks, t, n_pairs, 2, S5_GROUP)
    ut = jnp.transpose(ut, (3, 1, 0, 4, 2, 5)).reshape(n_pairs, rows, width)
    n_row_chunks = next(n for n in (8, 4, 2, 1) if rows % (8 * n) == 0)
    row_chunk = rows // n_row_chunks
    in_specs = [pl.BlockSpec((None, rows, width), lambda p: (p, 0, 0))]
    shapes = [(width, width)] * 2 + [(width, LANES)] * 4 + [(2, LANES)] + [(LANES, width)] * 4
    in_specs += [pl.BlockSpec((None,) + s, lambda p: (p, 0, 0)) for s in shapes]
    y = pl.pallas_call(
        functools.partial(_s5_kernel, bsz=bsz, row_chunk=row_chunk),
        out_shape=jax.ShapeDtypeStruct((n_pairs, rows, width), F32),
        grid=(n_pairs,),
        in_specs=in_specs,
        out_specs=pl.BlockSpec((None, rows, width), lambda p: (p, 0, 0)),
        scratch_shapes=[pltpu.VMEM((rows, LANES), F32)] * 4,
        compiler_params=_cparams(1),
        name="s5_chunked",
    )(ut, *ops)
    y = y.reshape(n_pairs, n_chunks, bsz, 2, t, S5_GROUP)
    return jnp.transpose(y, (2, 1, 4, 0, 3, 5)).reshape(bsz, seq, ch)


def _post_kernel(*refs, n_pad, ff_chunk, glu, final):
    h_ref, a_ref, b_ref = refs[:3]
    pos = 3
    if glu:
        wglu_ref, bglu_ref = refs[pos:pos + 2]
        pos += 2
    wo_ref, gf_ref, wgu_ref, wd_ref = refs[pos:pos + 4]
    pos += 4
    if final:
        gl_ref = refs[pos]
        pos += 1
    o_ref, acc_ref = refs[pos:pos + 2]

    tm = h_ref.shape[0]
    half = a_ref.shape[-1]
    d_ff = wd_ref.shape[0]
    row = lax.broadcasted_iota(jnp.int32, (tm, 1), 0) + pl.program_id(1) * tm
    valid = row >= n_pad

    a = a_ref[...]
    if glu:
        y = b_ref[...]
        z = _dot(y.astype(BF16), wglu_ref[...]) + bglu_ref[...]
        b = (y * (1.0 / (1.0 + jnp.exp(-z)))).astype(BF16)
    else:
        b = b_ref[...]
    mix = _dot(a, wo_ref[:half, :]) + _dot(b, wo_ref[half:, :])
    h1 = jnp.where(valid, h_ref[...] + mix, 0.0)
    hn = _rms(h1, gf_ref[...]).astype(BF16)
    acc_ref[...] = jnp.zeros_like(acc_ref)
    for c in range(0, d_ff, ff_chunk):
        gate = _dot(hn, wgu_ref[:, c:c + ff_chunk])
        up = _dot(hn, wgu_ref[:, d_ff + c:d_ff + c + ff_chunk])
        act = (gate * (1.0 / (1.0 + jnp.exp(-gate))) * up).astype(BF16)
        acc_ref[...] += _dot(act, wd_ref[c:c + ff_chunk, :])
    h2 = jnp.where(valid, h1 + acc_ref[...], 0.0)
    if final:
        h2 = _rms(h2, gl_ref[...])
    o_ref[...] = h2


def _post_mixer(h, a, b, w_out, g_ffn, w_gu, w_down, glu=None, final_g=None):
    bsz, seq, d = h.shape
    tm = next(t for t in (768, 512, 256) if seq % t == 0)
    half = a.shape[-1]
    d_ff = w_down.shape[0]
    row_spec = lambda w: pl.BlockSpec((None, tm, w), lambda bi, i: (bi, i, 0))
    const = lambda shape: _resident(shape, lambda bi, i: (0,) * len(shape))
    args = [h, a, b]
    in_specs = [row_spec(d), row_spec(half), row_spec(half)]
    if glu is not None:
        args += list(glu)
        in_specs += [const((half, half)), const((1, half))]
    args += [w_out, g_ffn, w_gu, w_down]
    in_specs += [const((2 * half, d)), const((1, d)), const((d, 2 * d_ff)), const((d_ff, d))]
    if final_g is not None:
        args.append(final_g)
        in_specs.append(const((1, d)))
    return pl.pallas_call(
        functools.partial(_post_kernel, n_pad=N_PAD, ff_chunk=256,
                          glu=glu is not None, final=final_g is not None),
        out_shape=jax.ShapeDtypeStruct((bsz, seq, d), F32),
        grid=(bsz, seq // tm),
        in_specs=in_specs,
        out_specs=row_spec(d),
        scratch_shapes=[pltpu.VMEM((tm, d), F32)],
        compiler_params=_cparams(2),
        name="outproj_ffn",
    )(*args)


def kernel(x, meta_tokens, norm_mix_g, norm_ffn_g, final_norm_g, even_w_in, even_w_out,
           diff_lam_q1, diff_lam_k1, diff_lam_q2, diff_lam_k2, diff_subln_g, odd_w_in,
           odd_w_out, fox_b_f, s5_lam_re, s5_lam_im, s5_log_dt, s5_b_re, s5_b_im, s5_c_re,
           s5_c_im, s5_d, s5_w_glu, s5_b_glu, ffn_w_gate_up, ffn_w_down):
    bsz, n_seq, d = x.shape
    seq = n_seq + FRONT
    half = d // 2
    n_blk = half // LANES
    scale = HEAD_DIM ** -0.5
    fox_heads = half // HEAD_DIM

    pad = jnp.zeros((bsz, N_PAD, d), x.dtype)
    meta = jnp.broadcast_to(meta_tokens[None].astype(x.dtype), (bsz, N_META, d))
    h = jnp.concatenate([pad, meta, x], axis=1)

    w_in = even_w_in[0]
    col_scale = jnp.ones((w_in.shape[1],), F32)
    col_scale = col_scale.at[:half].set(scale).at[3 * half:4 * half].set(scale)
    w_in = (w_in * col_scale[None, :]).astype(BF16)
    proj = _norm_proj_even(h.reshape(bsz * seq, d), norm_mix_g[0][None, :], w_in)
    proj = proj.reshape(bsz, seq, -1)
    sb = _sb_attention(proj, bsz, seq, 0, n_blk, 2 * n_blk, n_blk)
    lam_init = 0.8 - 0.6 * math.exp(-0.3 * 0)
    lam = (jnp.exp(jnp.sum(diff_lam_q1[0].astype(F32) * diff_lam_k1[0].astype(F32)))
           - jnp.exp(jnp.sum(diff_lam_q2[0].astype(F32) * diff_lam_k2[0].astype(F32))) + lam_init)
    df = _softmax_attention("diff", proj, bsz, seq, 3 * n_blk, 4 * n_blk, 5 * n_blk, n_blk,
                            (lam.reshape(1, 1).astype(F32), diff_subln_g[0][None, :].astype(F32)),
                            out_scale=1.0 - lam_init)
    h = _post_mixer(h, sb, df, even_w_out[0].astype(BF16), norm_ffn_g[0][None, :],
                    ffn_w_gate_up[0].astype(BF16), ffn_w_down[0].astype(BF16))

    w_in = odd_w_in[0]
    qkv_scale = jnp.ones((3 * half,), F32).at[:half].set(scale)
    w_qkv = (w_in[:, :3 * half] * qkv_scale[None, :]).astype(BF16)
    w_f = w_in[:, 3 * half:3 * half + fox_heads]
    w_u = w_in[:, 3 * half + fox_heads:].astype(BF16)
    f_rows = 16
    w_ft = jnp.zeros((f_rows, d), F32).at[:fox_heads].set(w_f.T)
    wfh, wfl = _split_bf16(w_ft)
    qkv, u, ft = _norm_proj_odd(h.reshape(bsz * seq, d), norm_mix_g[1][None, :],
                                w_qkv, w_u, wfh, wfl)
    qkv = qkv.reshape(bsz, seq, -1)
    ft = jnp.transpose(ft.reshape(f_rows, bsz, seq), (1, 0, 2))
    b_f = jnp.zeros((f_rows, 1), F32).at[:fox_heads, 0].set(fox_b_f[0].astype(F32))
    f_row = _forget_cumsum(ft, b_f)[:, :fox_heads].reshape(bsz, n_blk, 2, seq)
    f_col = jnp.transpose(f_row, (0, 1, 3, 2))
    fox = _softmax_attention("fox", qkv, bsz, seq, 0, n_blk, 2 * n_blk, n_blk, (f_col, f_row))
    ops = _s5_operators(s5_lam_re[0], s5_lam_im[0], s5_log_dt[0], s5_b_re[0], s5_b_im[0],
                        s5_c_re[0], s5_c_im[0], s5_d[0])
    ssm = _s5_layer(u.reshape(bsz, seq, -1), ops)
    h = _post_mixer(h, fox, ssm, odd_w_out[0].astype(BF16), norm_ffn_g[1][None, :],
                    ffn_w_gate_up[1].astype(BF16), ffn_w_down[1].astype(BF16),
                    glu=(s5_w_glu[0].astype(BF16), s5_b_glu[0][None, :].astype(F32)),
                    final_g=final_norm_g[None, :])
    return h[:, FRONT:]
```

```python
import functools
import math

import jax
import jax.numpy as jnp
from jax import lax
from jax.experimental import pallas as pl
from jax.experimental.pallas import tpu as pltpu

F32 = jnp.float32
BF16 = jnp.bfloat16

HEAD_DIM = 64
LANES = 128
ATT_BLOCK = 256
ATT_HEADS = ATT_BLOCK // HEAD_DIM
BF16_ROWS = 16
N_META = 16
ATT_TILE = 256
ATT_KTILE = 768
SB_BLOCK = 128
FRONT = ATT_TILE
N_PAD = FRONT - N_META
NEG = -1e30
RMS_EPS = 1e-6
S5_GROUP = 16
S5_STATE = 64
S5_T = 16
VMEM_LIMIT = 56 * 1024 * 1024


def _cparams(n_axes):
    return pltpu.CompilerParams(
        dimension_semantics=("arbitrary",) * n_axes, vmem_limit_bytes=VMEM_LIMIT)


def _resident(shape, index_map):
    return pl.BlockSpec(shape, index_map, pipeline_mode=pl.Buffered(1))


def _trunc_bf16(x):
    bits = lax.bitcast_convert_type(x, jnp.uint32) & jnp.uint32(0xFFFF0000)
    return lax.bitcast_convert_type(bits, F32)


def _split_bf16(w):
    hi = _trunc_bf16(w)
    return hi.astype(BF16), (w - hi).astype(BF16)


def _dot(a, b):
    return jnp.dot(a, b, preferred_element_type=F32)


def _dot_nt(a, b):
    return lax.dot_general(a, b, (((1,), (1,)), ((), ())), preferred_element_type=F32)


def _dot3(a_hi, a_lo, b_hi, b_lo):
    return _dot(a_hi, b_hi) + _dot(a_lo, b_hi) + _dot(a_hi, b_lo)


def _rms(x, g):
    ms = jnp.mean(x * x, axis=-1, keepdims=True)
    return x * lax.rsqrt(ms + RMS_EPS) * g


def _proj_kernel(*refs, n_chunk, odd):
    if odd:
        (x_ref, g_ref, wtok_ref, wtr_ref, wu_ref, wfh_ref, wfl_ref,
         tok_ref, tr_ref, u_ref, ft_ref) = refs
    else:
        x_ref, g_ref, wtok_ref, wtr_ref, tok_ref, tr_ref = refs
    hn32 = _rms(x_ref[...], g_ref[...])
    hn = hn32.astype(BF16)
    for c in range(0, tok_ref.shape[-1], n_chunk):
        tok_ref[:, c:c + n_chunk] = _dot(hn, wtok_ref[:, c:c + n_chunk]).astype(tok_ref.dtype)
    for c in range(0, tr_ref.shape[0], n_chunk):
        tr_ref[c:c + n_chunk, :] = _dot_nt(wtr_ref[c:c + n_chunk, :], hn).astype(tr_ref.dtype)
    if odd:
        u_ref[...] = _dot(hn, wu_ref[...])
        hn_lo = (hn32 - hn.astype(F32)).astype(BF16)
        wfh = wfh_ref[...]
        ft_ref[...] = _dot_nt(wfh, hn) + _dot_nt(wfh, hn_lo) + _dot_nt(wfl_ref[...], hn)


def _norm_proj(h, g, w_tok, w_tr, odd_weights=None):
    m, d = h.shape
    tm = next(t for t in (512, 256) if m % t == 0)
    n_tok, n_tr = w_tok.shape[1], w_tr.shape[0]
    const = lambda shape: _resident(shape, lambda i: (0, 0))
    args = [h, g, w_tok, w_tr]
    in_specs = [pl.BlockSpec((tm, d), lambda i: (i, 0)), const((1, d)),
                const((d, n_tok)), const((n_tr, d))]
    out_shape = [jax.ShapeDtypeStruct((m, n_tok), BF16), jax.ShapeDtypeStruct((n_tr, m), BF16)]
    out_specs = [pl.BlockSpec((tm, n_tok), lambda i: (i, 0)),
                 pl.BlockSpec((n_tr, tm), lambda i: (0, i))]
    if odd_weights is not None:
        wu, wfh, wfl = odd_weights
        nu, nf = wu.shape[1], wfh.shape[0]
        args += [wu, wfh, wfl]
        in_specs += [const((d, nu)), const((nf, d)), const((nf, d))]
        out_shape += [jax.ShapeDtypeStruct((m, nu), F32), jax.ShapeDtypeStruct((nf, m), F32)]
        out_specs += [pl.BlockSpec((tm, nu), lambda i: (i, 0)),
                      pl.BlockSpec((nf, tm), lambda i: (0, i))]
    return pl.pallas_call(
        functools.partial(_proj_kernel, n_chunk=256, odd=odd_weights is not None),
        out_shape=tuple(out_shape),
        grid=(m // tm,),
        in_specs=in_specs,
        out_specs=tuple(out_specs),
        compiler_params=_cparams(1),
        name="norm_proj_odd" if odd_weights is not None else "norm_proj_even",
    )(*args)


def _head_rows(h):
    return slice(h * HEAD_DIM, (h + 1) * HEAD_DIM)


def _head_queries(qt):
    zero = jnp.zeros((HEAD_DIM, qt.shape[1]), qt.dtype)
    return [jnp.concatenate([qt[_head_rows(r)] if r == h else zero for r in range(ATT_HEADS)],
                            axis=0) for h in range(ATT_HEADS)]


def _tile_mask(kind, start, off, tq, tk, n_pad, strict):
    if kind == "mid":
        return None
    row = lax.broadcasted_iota(jnp.int32, (tk, 1), 0)
    if kind == "first":
        return row >= n_pad
    dist = (lax.broadcasted_iota(jnp.int32, (tk, tq), 0)
            - lax.broadcasted_iota(jnp.int32, (tk, tq), 1))
    causal = (dist < off) if strict else (dist <= off)
    return causal & (row + start >= n_pad)


def _sb_kernel(k_ref, qt_ref, vt_ref, tri_ref, o_ref, acc_ref, carry_ref, *, tq, tk, blk, n_pad):
    i = pl.program_id(2)
    ratio = tk // tq
    c_diag = i // ratio
    off = (i - c_diag * ratio) * tq
    n_blocks = tk // blk
    q_cat = jnp.concatenate(_head_queries(qt_ref[...]), axis=1)
    tri = tri_ref[...]
    acc_ref[...] = jnp.zeros_like(acc_ref)
    carry_ref[...] = jnp.zeros_like(carry_ref)

    def step(c, kind):
        start = pl.multiple_of(c * tk, tk)
        ks = k_ref[pl.ds(start, tk), :]
        vt = vt_ref[:, pl.ds(start, tk)]
        mask = _tile_mask(kind, start, off, tq, tk, n_pad, strict=True)
        logits = _dot(ks, q_cat)
        for h in range(ATT_HEADS):
            logit = logits[:, h * tq:(h + 1) * tq]
            sp = jnp.maximum(logit, 0.0) + jnp.log(1.0 + jnp.exp(-jnp.abs(logit)))
            log_beta = logit - sp
            log_1m = -sp
            if mask is not None:
                log_1m = jnp.where(mask, log_1m, 0.0)
            hi = log_1m.astype(BF16)
            lo = (log_1m - hi.astype(F32)).astype(BF16)
            blocks = [slice(b * blk, (b + 1) * blk) for b in range(n_blocks)]
            within = _dot(tri, jnp.concatenate(
                [jnp.concatenate([hi[r], lo[r]], axis=0) for r in blocks], axis=1))
            run = carry_ref[h]
            w_blocks = [None] * n_blocks
            for b in reversed(range(n_blocks)):
                r = blocks[b]
                w_blocks[b] = jnp.exp(log_beta[r] + within[:, b * tq:(b + 1) * tq] + run)
                run = run + jnp.sum(log_1m[r], axis=0, keepdims=True)
            carry_ref[h] = run
            w = jnp.concatenate(w_blocks, axis=0)
            if mask is not None:
                w = jnp.where(mask, w, 0.0)
            rows = _head_rows(h)
            acc_ref[rows, :] += _dot(vt[rows, :], w.astype(BF16))

    step(c_diag, "diag")

    def mid(n, carry):
        step(c_diag - n, "mid")
        return carry

    lax.fori_loop(1, c_diag, mid, 0)

    @pl.when(c_diag > 0)
    def _():
        step(0, "first")

    o_ref[...] = acc_ref[...].T.astype(o_ref.dtype)


def _fox_queries(qt, qb):
    tq = qt.shape[1]
    zeros = lambda n: jnp.zeros((n, tq), qt.dtype)
    out = []
    for h in range(ATT_HEADS):
        terms = qb[h * BF16_ROWS:(h + 1) * BF16_ROWS]
        if h % 2 == 0:
            pieces = [qt[_head_rows(h)], zeros(HEAD_DIM), terms, zeros(2 * HEAD_DIM - BF16_ROWS)]
        else:
            pieces = [zeros(HEAD_DIM), qt[_head_rows(h)], zeros(BF16_ROWS), terms,
                      zeros(2 * HEAD_DIM - 2 * BF16_ROWS)]
        out.append(jnp.concatenate(pieces, axis=0))
    return out


def _softmax_attn_kernel(*refs, tq, tk, n_pad, mode, out_scale):
    if mode == "fox":
        k_ref, qt_ref, qb_ref, vt_ref, o_ref, acc_ref, m_ref = refs
        q_heads = _fox_queries(qt_ref[...], qb_ref[...])
        q_cats = [jnp.concatenate(q_heads[2 * p:2 * p + 2], axis=1) for p in range(ATT_HEADS // 2)]
    else:
        k_ref, qt_ref, vt_ref, lam_ref, g_ref, o_ref, acc_ref, m_ref = refs
        q_cats = [jnp.concatenate(_head_queries(qt_ref[...]), axis=1)]
    i = pl.program_id(2)
    ratio = tk // tq
    c_diag = i // ratio
    off = (i - c_diag * ratio) * tq
    ones_rows = (lax.broadcasted_iota(jnp.int32, (BF16_ROWS, tk), 0) == 0).astype(BF16)
    v_rows = HEAD_DIM if mode == "fox" else 2 * HEAD_DIM
    acc_ref[...] = jnp.zeros_like(acc_ref)
    m_ref[...] = jnp.full_like(m_ref, NEG)

    def step(c, kind):
        start = pl.multiple_of(c * tk, tk)
        ks = k_ref[pl.ds(start, tk), :]
        vt = vt_ref[:, pl.ds(start, tk)]
        mask = _tile_mask(kind, start, off, tq, tk, n_pad, strict=False)
        scores = [_dot(ks[:, n * ATT_BLOCK:(n + 1) * ATT_BLOCK], q) for n, q in enumerate(q_cats)]
        per_cat = ATT_HEADS // len(q_cats)
        for h in range(ATT_HEADS):
            col = (h % per_cat) * tq
            s = scores[h // per_cat][:, col:col + tq]
            if mask is not None:
                s = jnp.where(mask, s, NEG)
            m_old = m_ref[h]
            m_new = jnp.maximum(m_old, jnp.max(s, axis=0, keepdims=True))
            alpha = jnp.exp(m_old - m_new)
            p = jnp.exp(s - m_new).astype(BF16)
            m_ref[h] = m_new
            vh = vt[_head_rows(h)] if mode == "fox" else vt[(h // 2) * v_rows:(h // 2 + 1) * v_rows]
            v_aug = jnp.concatenate([vh, ones_rows], axis=0)
            acc_ref[h] = alpha * acc_ref[h] + _dot(v_aug, p)

    @pl.when(c_diag > 0)
    def _():
        step(0, "first")

    def mid(c, carry):
        step(c, "mid")
        return carry

    lax.fori_loop(1, c_diag, mid, 0)
    step(c_diag, "diag")

    outs = []
    for h in range(ATT_HEADS):
        acc = acc_ref[h]
        outs.append(acc[:v_rows] * (1.0 / acc[v_rows:v_rows + 1]))
    if mode == "diff":
        heads = []
        for d in range(ATT_HEADS // 2):
            out_d = outs[2 * d] - lam_ref[0, 0] * outs[2 * d + 1]
            ms = jnp.mean(out_d * out_d, axis=0, keepdims=True)
            heads.append(out_d * lax.rsqrt(ms + RMS_EPS) * g_ref[...] * out_scale)
        outs = heads
    o_ref[...] = jnp.concatenate(outs, axis=0).T.astype(o_ref.dtype)


def _attention(mode, tok, tr, bsz, seq, k_blk, q_blk, v_blk, n_blk, extra=(), out_scale=1.0):
    tq, tk = ATT_TILE, ATT_KTILE
    nq = seq // tq
    k_width = 2 * ATT_BLOCK if mode == "fox" else ATT_BLOCK
    k_spec = pl.BlockSpec((seq, k_width), lambda b, p, i: (b, k_blk + p))
    qt_spec = pl.BlockSpec((ATT_BLOCK, tq), lambda b, p, i: (q_blk + p, b * nq + i))
    vt_spec = pl.BlockSpec((ATT_BLOCK, seq), lambda b, p, i: (v_blk + p, b))
    m_scratch = pltpu.VMEM((ATT_HEADS, 1, tq), F32)
    if mode == "sb":
        idx = jnp.arange(SB_BLOCK)
        tri = (idx[None, :] > idx[:, None]).astype(BF16)
        args = [tok, tr, tr, jnp.concatenate([tri, tri], axis=1)]
        in_specs = [k_spec, qt_spec, vt_spec,
                    _resident((SB_BLOCK, 2 * SB_BLOCK), lambda b, p, i: (0, 0))]
        scratch = [pltpu.VMEM((ATT_BLOCK, tq), F32), m_scratch]
        body = functools.partial(_sb_kernel, tq=tq, tk=tk, blk=SB_BLOCK, n_pad=N_PAD)
    elif mode == "fox":
        args = [tok, tr, extra[0], tr]
        in_specs = [k_spec, qt_spec,
                    pl.BlockSpec((None, None, ATT_HEADS * BF16_ROWS, tq), lambda b, p, i: (b, p, 0, i)),
                    vt_spec]
        scratch = [pltpu.VMEM((ATT_HEADS, HEAD_DIM + BF16_ROWS, tq), F32), m_scratch]
        body = functools.partial(_softmax_attn_kernel, tq=tq, tk=tk, n_pad=N_PAD, mode=mode,
                                 out_scale=out_scale)
    else:
        lam, g_col = extra
        args = [tok, tr, tr, lam, g_col]
        in_specs = [k_spec, qt_spec, vt_spec, pl.BlockSpec(memory_space=pltpu.SMEM),
                    _resident((2 * HEAD_DIM, tq), lambda b, p, i: (0, 0))]
        scratch = [pltpu.VMEM((ATT_HEADS, 2 * HEAD_DIM + BF16_ROWS, tq), F32), m_scratch]
        body = functools.partial(_softmax_attn_kernel, tq=tq, tk=tk, n_pad=N_PAD, mode=mode,
                                 out_scale=out_scale)
    return pl.pallas_call(
        body,
        out_shape=jax.ShapeDtypeStruct((bsz * seq, n_blk * ATT_BLOCK), BF16),
        grid=(bsz, n_blk, nq),
        in_specs=in_specs,
        out_specs=pl.BlockSpec((tq, ATT_BLOCK), lambda b, p, i: (b * nq + i, p)),
        scratch_shapes=scratch,
        compiler_params=_cparams(3),
        name=mode + "_attention",
    )(*args)


def _fcum_kernel(ft_ref, b_ref, tri_ref, o_ref, *, tile, n_pad):
    seq = ft_ref.shape[-1]
    tri = tri_ref[...]
    carry = jnp.zeros((ft_ref.shape[0], 1), F32)
    for c in range(seq // tile):
        x = ft_ref[:, c * tile:(c + 1) * tile] + b_ref[...]
        log_f = jnp.minimum(x, 0.0) - jnp.log(1.0 + jnp.exp(-jnp.abs(x)))
        pos = lax.broadcasted_iota(jnp.int32, (1, tile), 1) + c * tile
        log_f = jnp.where(pos >= n_pad, log_f, 0.0)
        hi = log_f.astype(BF16)
        r1 = log_f - hi.astype(F32)
        mid = r1.astype(BF16)
        lo = (r1 - mid.astype(F32)).astype(BF16)
        cs = _dot(jnp.concatenate([hi, mid, lo], axis=1), tri)
        o_ref[:, c * tile:(c + 1) * tile] = cs + carry
        carry = carry + jnp.sum(log_f, axis=1, keepdims=True)


def _forget_cumsum(ft, b_f):
    bsz, rows, seq = ft.shape
    tile = ATT_TILE
    idx = jnp.arange(tile)
    tri = (idx[:, None] <= idx[None, :]).astype(BF16)
    tri3 = jnp.concatenate([tri, tri, tri], axis=0)
    return pl.pallas_call(
        functools.partial(_fcum_kernel, tile=tile, n_pad=N_PAD),
        out_shape=jax.ShapeDtypeStruct((bsz, rows, seq), F32),
        grid=(bsz,),
        in_specs=[pl.BlockSpec((None, rows, seq), lambda b: (b, 0, 0)),
                  _resident((rows, 1), lambda b: (0, 0)),
                  _resident((3 * tile, tile), lambda b: (0, 0))],
        out_specs=pl.BlockSpec((None, rows, seq), lambda b: (b, 0, 0)),
        compiler_params=_cparams(1),
        name="forget_cumsum",
    )(ft, b_f, tri3)


def _forget_bias_operands(f_row, keys, n_blk):
    bsz, heads, seq = f_row.shape
    hi = _trunc_bf16(f_row)
    mid = _trunc_bf16(f_row - hi)
    lo = ((f_row - hi) - mid).astype(BF16)
    hi, mid = hi.astype(BF16), mid.astype(BF16)
    one = jnp.ones_like(hi)
    zero = jnp.zeros((bsz, heads, BF16_ROWS - 6, seq), BF16)
    k_terms = jnp.concatenate([jnp.stack([hi, mid, lo, one, one, one], axis=2), zero], axis=2)
    q_terms = jnp.concatenate([jnp.stack([-one, -one, -one, hi, mid, lo], axis=2), zero], axis=2)
    pairs = heads // 2
    k_terms = jnp.transpose(k_terms.reshape(bsz, pairs, 2 * BF16_ROWS, seq), (0, 3, 1, 2))
    k_ext = jnp.concatenate(
        [keys.reshape(bsz, seq, pairs, 2 * HEAD_DIM), k_terms,
         jnp.zeros((bsz, seq, pairs, 2 * HEAD_DIM - 2 * BF16_ROWS), BF16)], axis=3)
    q_bias = q_terms.reshape(bsz, n_blk, ATT_HEADS * BF16_ROWS, seq)
    return k_ext.reshape(bsz * seq, pairs * ATT_BLOCK), q_bias


def _s5_kernel(u_ref, toep_h, toep_l, sbr_h, sbr_l, sbi_h, sbi_l, at_ref,
               car_h, car_l, cai_h, cai_l, y_ref, sr_ref, si_ref, xr_ref, xi_ref,
               *, bsz, row_chunk):
    rows = u_ref.shape[0]
    n_chunks = rows // bsz

    def split(x):
        hi = x.astype(BF16)
        return hi, (x - hi.astype(F32)).astype(BF16)

    def stage1(r, c):
        sl = pl.ds(pl.multiple_of(r * row_chunk, 8), row_chunk)
        uh, ul = split(u_ref[sl, :])
        y_ref[sl, :] = _dot3(uh, ul, toep_h[...], toep_l[...])
        sr_ref[sl, :] = _dot3(uh, ul, sbr_h[...], sbr_l[...])
        si_ref[sl, :] = _dot3(uh, ul, sbi_h[...], sbi_l[...])
        return c

    lax.fori_loop(0, rows // row_chunk, stage1, 0)

    a_re = at_ref[0:1, :]
    a_im = at_ref[1:2, :]

    def scan(c, state):
        x_re, x_im = state
        sl = pl.ds(c * bsz, bsz)
        xr_ref[sl, :] = x_re
        xi_ref[sl, :] = x_im
        n_re = a_re * x_re - a_im * x_im + sr_ref[sl, :]
        n_im = a_re * x_im + a_im * x_re + si_ref[sl, :]
        return n_re, n_im

    zero = jnp.zeros((bsz, LANES), F32)
    lax.fori_loop(0, n_chunks, scan, (zero, zero))

    def stage3(r, c):
        sl = pl.ds(pl.multiple_of(r * row_chunk, 8), row_chunk)
        xrh, xrl = split(xr_ref[sl, :])
        xih, xil = split(xi_ref[sl, :])
        y = (y_ref[sl, :] + _dot3(xrh, xrl, car_h[...], car_l[...])
             + _dot3(xih, xil, cai_h[...], cai_l[...]))
        y_ref[sl, :] = 0.5 * y * (1.0 + jnp.tanh(
            math.sqrt(2.0 / math.pi) * (y + 0.044715 * (y * y * y))))
        return c

    lax.fori_loop(0, rows // row_chunk, stage3, 0)


def _s5_operators(lam_re, lam_im, log_dt, b_re, b_im, c_re, c_im, d_skip):
    t = S5_T
    lre, lim = lam_re.astype(F32), lam_im.astype(F32)
    dt = jnp.exp(log_dt.astype(F32))[:, None]
    mag = jnp.exp(lre * dt)
    a_re, a_im = mag * jnp.cos(lim * dt), mag * jnp.sin(lim * dt)
    den = lre * lre + lim * lim
    g_re = ((a_re - 1.0) * lre + a_im * lim) / den
    g_im = (a_im * lre - (a_re - 1.0) * lim) / den
    br, bi = b_re.astype(F32), b_im.astype(F32)
    bb_re = g_re[..., None] * br - g_im[..., None] * bi
    bb_im = g_re[..., None] * bi + g_im[..., None] * br
    k = jnp.arange(t + 1, dtype=F32)[:, None, None]
    pw_mag = jnp.exp(k * (lre * dt)[None])
    pw_re = pw_mag * jnp.cos(k * (lim * dt)[None])
    pw_im = pw_mag * jnp.sin(k * (lim * dt)[None])
    cr, ci = c_re.astype(F32), c_im.astype(F32)
    hp = lax.Precision.HIGHEST
    ab_re = pw_re[..., None] * bb_re[None] - pw_im[..., None] * bb_im[None]
    ab_im = pw_re[..., None] * bb_im[None] + pw_im[..., None] * bb_re[None]
    kern = (jnp.einsum('ghp,kgpe->kghe', cr, ab_re, precision=hp)
            - jnp.einsum('ghp,kgpe->kghe', ci, ab_im, precision=hp))
    lag = jnp.arange(t)[None, :] - jnp.arange(t)[:, None]
    toep = jnp.where((lag >= 0)[:, :, None, None, None],
                     kern[jnp.clip(lag, 0, t)], 0.0)
    eye_t = jnp.eye(t, dtype=F32)
    eye_h = jnp.eye(S5_GROUP, dtype=F32)
    toep = toep + (eye_t[:, :, None, None, None] * eye_h[None, None, None]
                   * d_skip.astype(F32)[None, None, :, :, None])
    toep = jnp.transpose(toep, (2, 0, 4, 1, 3))
    g_n = toep.shape[0]
    toep = toep.reshape(g_n, t * S5_GROUP, t * S5_GROUP)
    rev = t - 1 - jnp.arange(t)
    sb_re = jnp.transpose(ab_re[rev], (1, 0, 3, 2)).reshape(g_n, t * S5_GROUP, S5_STATE)
    sb_im = jnp.transpose(ab_im[rev], (1, 0, 3, 2)).reshape(g_n, t * S5_GROUP, S5_STATE)
    e_re, e_im = pw_re[1:], pw_im[1:]
    ca_re = cr[None] * e_re[:, :, None, :] - ci[None] * e_im[:, :, None, :]
    ca_im = -(cr[None] * e_im[:, :, None, :] + ci[None] * e_re[:, :, None, :])
    ca_re = jnp.transpose(ca_re, (1, 3, 0, 2)).reshape(g_n, S5_STATE, t * S5_GROUP)
    ca_im = jnp.transpose(ca_im, (1, 3, 0, 2)).reshape(g_n, S5_STATE, t * S5_GROUP)
    at = jnp.stack([pw_re[t], pw_im[t]], axis=1)

    def pair_diag(x):
        g2 = x.reshape(g_n // 2, 2, *x.shape[1:])
        z = jnp.zeros_like(g2[:, 0])
        return jnp.concatenate([jnp.concatenate([g2[:, 0], z], axis=2),
                                jnp.concatenate([z, g2[:, 1]], axis=2)], axis=1)

    at_pair = at.reshape(g_n // 2, 2, 2, S5_STATE).transpose(0, 2, 1, 3).reshape(
        g_n // 2, 2, 2 * S5_STATE)
    ops = [pair_diag(toep), pair_diag(sb_re), pair_diag(sb_im)]
    ops2 = [pair_diag(ca_re), pair_diag(ca_im)]
    out = []
    for x in ops:
        out += list(_split_bf16(x))
    out.append(at_pair)
    for x in ops2:
        out += list(_split_bf16(x))
    return out


def _s5_layer(u, ops):
    bsz, seq, ch = u.shape
    t = S5_T
    n_pairs = ch // (2 * S5_GROUP)
    n_chunks = seq // t
    rows = n_chunks * bsz
    width = 2 * t * S5_GROUP
    ut = u.reshape(bsz, n_chunks, t, n_pairs, 2, S5_GROUP)
    ut = jnp.transpose(ut, (3, 1, 0, 4, 2, 5)).reshape(n_pairs, rows, width)
    n_row_chunks = next(n for n in (8, 4, 2, 1) if rows % (8 * n) == 0)
    row_chunk = rows // n_row_chunks
    in_specs = [pl.BlockSpec((None, rows, width), lambda p: (p, 0, 0))]
    shapes = [(width, width)] * 2 + [(width, LANES)] * 4 + [(2, LANES)] + [(LANES, width)] * 4
    in_specs += [pl.BlockSpec((None,) + s, lambda p: (p, 0, 0)) for s in shapes]
    y = pl.pallas_call(
        functools.partial(_s5_kernel, bsz=bsz, row_chunk=row_chunk),
        out_shape=jax.ShapeDtypeStruct((n_pairs, rows, width), F32),
        grid=(n_pairs,),
        in_specs=in_specs,
        out_specs=pl.BlockSpec((None, rows, width), lambda p: (p, 0, 0)),
        scratch_shapes=[pltpu.VMEM((rows, LANES), F32)] * 4,
        compiler_params=_cparams(1),
        name="s5_chunked",
    )(ut, *ops)
    y = y.reshape(n_pairs, n_chunks, bsz, 2, t, S5_GROUP)
    return jnp.transpose(y, (2, 1, 4, 0, 3, 5)).reshape(bsz, seq, ch)


def _post_kernel(*refs, n_pad, ff_chunk, glu, final):
    h_ref, a_ref, b_ref = refs[:3]
    pos = 3
    if glu:
        wglu_ref, bglu_ref = refs[pos:pos + 2]
        pos += 2
    wo_ref, gf_ref, wgu_ref, wd_ref = refs[pos:pos + 4]
    pos += 4
    if final:
        gl_ref = refs[pos]
        pos += 1
    o_ref, acc_ref = refs[pos:pos + 2]

    tm = h_ref.shape[0]
    half = a_ref.shape[-1]
    d_ff = wd_ref.shape[0]
    row = lax.broadcasted_iota(jnp.int32, (tm, 1), 0) + pl.program_id(1) * tm
    valid = row >= n_pad

    a = a_ref[...]
    if glu:
        y = b_ref[...]
        z = _dot(y.astype(BF16), wglu_ref[...]) + bglu_ref[...]
        b = (y * (1.0 / (1.0 + jnp.exp(-z)))).astype(BF16)
    else:
        b = b_ref[...]
    mix = _dot(a, wo_ref[:half, :]) + _dot(b, wo_ref[half:, :])
    h1 = jnp.where(valid, h_ref[...] + mix, 0.0)
    hn = _rms(h1, gf_ref[...]).astype(BF16)
    acc_ref[...] = jnp.zeros_like(acc_ref)
    for c in range(0, d_ff, ff_chunk):
        gate = _dot(hn, wgu_ref[:, c:c + ff_chunk])
        up = _dot(hn, wgu_ref[:, d_ff + c:d_ff + c + ff_chunk])
        act = (gate * (1.0 / (1.0 + jnp.exp(-gate))) * up).astype(BF16)
        acc_ref[...] += _dot(act, wd_ref[c:c + ff_chunk, :])
    h2 = jnp.where(valid, h1 + acc_ref[...], 0.0)
    if final:
        h2 = _rms(h2, gl_ref[...])
    o_ref[...] = h2


def _post_mixer(h, a, b, w_out, g_ffn, w_gu, w_down, glu=None, final_g=None):
    bsz, seq, d = h.shape
    tm = next(t for t in (768, 512, 256) if seq % t == 0)
    half = a.shape[-1]
    d_ff = w_down.shape[0]
    row_spec = lambda w: pl.BlockSpec((None, tm, w), lambda bi, i: (bi, i, 0))
    const = lambda shape: _resident(shape, lambda bi, i: (0,) * len(shape))
    args = [h, a, b]
    in_specs = [row_spec(d), row_spec(half), row_spec(half)]
    if glu is not None:
        args += list(glu)
        in_specs += [const((half, half)), const((1, half))]
    args += [w_out, g_ffn, w_gu, w_down]
    in_specs += [const((2 * half, d)), const((1, d)), const((d, 2 * d_ff)), const((d_ff, d))]
    if final_g is not None:
        args.append(final_g)
        in_specs.append(const((1, d)))
    return pl.pallas_call(
        functools.partial(_post_kernel, n_pad=N_PAD, ff_chunk=256,
                          glu=glu is not None, final=final_g is not None),
        out_shape=jax.ShapeDtypeStruct((bsz, seq, d), F32),
        grid=(bsz, seq // tm),
        in_specs=in_specs,
        out_specs=row_spec(d),
        scratch_shapes=[pltpu.VMEM((tm, d), F32)],
        compiler_params=_cparams(2),
        name="outproj_ffn",
    )(*args)


def kernel(x, meta_tokens, norm_mix_g, norm_ffn_g, final_norm_g, even_w_in, even_w_out,
           diff_lam_q1, diff_lam_k1, diff_lam_q2, diff_lam_k2, diff_subln_g, odd_w_in,
           odd_w_out, fox_b_f, s5_lam_re, s5_lam_im, s5_log_dt, s5_b_re, s5_b_im, s5_c_re,
           s5_c_im, s5_d, s5_w_glu, s5_b_glu, ffn_w_gate_up, ffn_w_down):
    bsz, n_seq, d = x.shape
    seq = n_seq + FRONT
    half = d // 2
    n_blk = half // ATT_BLOCK
    scale = HEAD_DIM ** -0.5
    fox_heads = half // HEAD_DIM

    pad = jnp.zeros((bsz, N_PAD, d), x.dtype)
    meta = jnp.broadcast_to(meta_tokens[None].astype(x.dtype), (bsz, N_META, d))
    h = jnp.concatenate([pad, meta, x], axis=1)

    w_in = even_w_in[0]
    col = lambda n: w_in[:, n * half:(n + 1) * half]
    w_tok = jnp.concatenate([col(1), col(4)], axis=1).astype(BF16)
    w_tr = jnp.concatenate([col(0) * scale, col(2), col(3) * scale, col(5)], axis=1).T.astype(BF16)
    tok, tr = _norm_proj(h.reshape(bsz * seq, d), norm_mix_g[0][None, :], w_tok, w_tr)
    sb = _attention("sb", tok, tr, bsz, seq, 0, 0, n_blk, n_blk)
    lam_init = 0.8 - 0.6 * math.exp(-0.3 * 0)
    lam = (jnp.exp(jnp.sum(diff_lam_q1[0].astype(F32) * diff_lam_k1[0].astype(F32)))
           - jnp.exp(jnp.sum(diff_lam_q2[0].astype(F32) * diff_lam_k2[0].astype(F32))) + lam_init)
    g_col = jnp.broadcast_to(diff_subln_g[0].astype(F32)[:, None], (2 * HEAD_DIM, ATT_TILE))
    df = _attention("diff", tok, tr, bsz, seq, n_blk, 2 * n_blk, 3 * n_blk, n_blk,
                    (lam.reshape(1, 1).astype(F32), g_col), out_scale=1.0 - lam_init)
    h = _post_mixer(h, sb.reshape(bsz, seq, half), df.reshape(bsz, seq, half),
                    even_w_out[0].astype(BF16), norm_ffn_g[0][None, :],
                    ffn_w_gate_up[0].astype(BF16), ffn_w_down[0].astype(BF16))

    w_in = odd_w_in[0]
    col = lambda n: w_in[:, n * half:(n + 1) * half]
    w_tok = col(1).astype(BF16)
    w_tr = jnp.concatenate([col(0) * scale, col(2)], axis=1).T.astype(BF16)
    w_f = w_in[:, 3 * half:3 * half + fox_heads]
    w_u = w_in[:, 3 * half + fox_heads:].astype(BF16)
    f_rows = BF16_ROWS
    w_ft = jnp.zeros((f_rows, d), F32).at[:fox_heads].set(w_f.T)
    tok, tr, u, ft = _norm_proj(h.reshape(bsz * seq, d), norm_mix_g[1][None, :], w_tok, w_tr,
                                odd_weights=(w_u,) + _split_bf16(w_ft))
    ft = jnp.transpose(ft.reshape(f_rows, bsz, seq), (1, 0, 2))
    b_f = jnp.zeros((f_rows, 1), F32).at[:fox_heads, 0].set(fox_b_f[0].astype(F32))
    f_row = _forget_cumsum(ft, b_f)[:, :fox_heads]
    k_ext, q_bias = _forget_bias_operands(f_row, tok, n_blk)
    fox = _attention("fox", k_ext, tr, bsz, seq, 0, 0, n_blk, n_blk, (q_bias,))
    ops = _s5_operators(s5_lam_re[0], s5_lam_im[0], s5_log_dt[0], s5_b_re[0], s5_b_im[0],
                        s5_c_re[0], s5_c_im[0], s5_d[0])
    ssm = _s5_layer(u.reshape(bsz, seq, -1), ops)
    h = _post_mixer(h, fox.reshape(bsz, seq, half), ssm, odd_w_out[0].astype(BF16),
                    norm_ffn_g[1][None, :], ffn_w_gate_up[1].astype(BF16),
                    ffn_w_down[1].astype(BF16),
                    glu=(s5_w_glu[0].astype(BF16), s5_b_glu[0][None, :].astype(F32)),
                    final_g=final_norm_g[None, :])
    return h[:, FRONT:]
```

```python
import functools
import math

import jax
import jax.numpy as jnp
from jax import lax
from jax.experimental import pallas as pl
from jax.experimental.pallas import tpu as pltpu

F32 = jnp.float32
BF16 = jnp.bfloat16

HEAD_DIM = 64
LANES = 128
ATT_BLOCK = 256
ATT_HEADS = ATT_BLOCK // HEAD_DIM
BF16_ROWS = 16
N_META = 16
ATT_TILE = 256
ATT_KTILE = 768
SB_BLOCK = 128
FRONT = ATT_TILE
N_PAD = FRONT - N_META
NEG = -1e30
RMS_EPS = 1e-6
LOG2_E = 1.4426950408889634
S5_GROUP = 16
S5_STATE = 64
S5_T = 8
S5_BLOCK_GROUPS = LANES // S5_GROUP
VMEM_LIMIT = 56 * 1024 * 1024


def _cparams(n_axes):
    return pltpu.CompilerParams(
        dimension_semantics=("arbitrary",) * n_axes, vmem_limit_bytes=VMEM_LIMIT)


def _resident(shape, index_map):
    return pl.BlockSpec(shape, index_map, pipeline_mode=pl.Buffered(1))


def _trunc_bf16(x):
    bits = lax.bitcast_convert_type(x, jnp.uint32) & jnp.uint32(0xFFFF0000)
    return lax.bitcast_convert_type(bits, F32)


def _split_bf16(w):
    hi = _trunc_bf16(w)
    return hi.astype(BF16), (w - hi).astype(BF16)


def _dot(a, b):
    return jnp.dot(a, b, preferred_element_type=F32)


def _dot_nt(a, b):
    return lax.dot_general(a, b, (((1,), (1,)), ((), ())), preferred_element_type=F32)


def _dot3(a_hi, a_lo, b_hi, b_lo):
    return _dot(a_hi, b_hi) + _dot(a_lo, b_hi) + _dot(a_hi, b_lo)


def _rms(x, g):
    ms = jnp.mean(x * x, axis=-1, keepdims=True)
    return x * lax.rsqrt(ms + RMS_EPS) * g


def _proj_kernel(*refs, n_chunk, odd):
    if odd:
        (x_ref, g_ref, wtok_ref, wtr_ref, wu_ref, wfh_ref, wfl_ref,
         tok_ref, tr_ref, u_ref, ft_ref) = refs
    else:
        x_ref, g_ref, wtok_ref, wtr_ref, tok_ref, tr_ref = refs
    hn32 = _rms(x_ref[...], g_ref[...])
    hn = hn32.astype(BF16)
    for c in range(0, tok_ref.shape[-1], n_chunk):
        tok_ref[:, c:c + n_chunk] = _dot(hn, wtok_ref[:, c:c + n_chunk]).astype(tok_ref.dtype)
    for c in range(0, tr_ref.shape[0], n_chunk):
        tr_ref[c:c + n_chunk, :] = _dot_nt(wtr_ref[c:c + n_chunk, :], hn).astype(tr_ref.dtype)
    if odd:
        u_ref[...] = _dot(hn, wu_ref[...])
        hn_lo = (hn32 - hn.astype(F32)).astype(BF16)
        wfh = wfh_ref[...]
        ft_ref[...] = _dot_nt(wfh, hn) + _dot_nt(wfh, hn_lo) + _dot_nt(wfl_ref[...], hn)


def _norm_proj(h, g, w_tok, w_tr, odd_weights=None):
    m, d = h.shape
    tm = next(t for t in (512, 256) if m % t == 0)
    n_tok, n_tr = w_tok.shape[1], w_tr.shape[0]
    const = lambda shape: _resident(shape, lambda i: (0, 0))
    args = [h, g, w_tok, w_tr]
    in_specs = [pl.BlockSpec((tm, d), lambda i: (i, 0)), const((1, d)),
                const((d, n_tok)), const((n_tr, d))]
    out_shape = [jax.ShapeDtypeStruct((m, n_tok), BF16), jax.ShapeDtypeStruct((n_tr, m), BF16)]
    out_specs = [pl.BlockSpec((tm, n_tok), lambda i: (i, 0)),
                 pl.BlockSpec((n_tr, tm), lambda i: (0, i))]
    if odd_weights is not None:
        wu, wfh, wfl = odd_weights
        nu, nf = wu.shape[1], wfh.shape[0]
        args += [wu, wfh, wfl]
        in_specs += [const((d, nu)), const((nf, d)), const((nf, d))]
        out_shape += [jax.ShapeDtypeStruct((m, nu), F32), jax.ShapeDtypeStruct((nf, m), F32)]
        out_specs += [pl.BlockSpec((tm, nu), lambda i: (i, 0)),
                      pl.BlockSpec((nf, tm), lambda i: (0, i))]
    return pl.pallas_call(
        functools.partial(_proj_kernel, n_chunk=256, odd=odd_weights is not None),
        out_shape=tuple(out_shape),
        grid=(m // tm,),
        in_specs=in_specs,
        out_specs=tuple(out_specs),
        compiler_params=_cparams(1),
        name="norm_proj_odd" if odd_weights is not None else "norm_proj_even",
    )(*args)


def _head_rows(h):
    return slice(h * HEAD_DIM, (h + 1) * HEAD_DIM)


def _head_queries(qt):
    zero = jnp.zeros((HEAD_DIM, qt.shape[1]), qt.dtype)
    return [jnp.concatenate([qt[_head_rows(r)] if r == h else zero for r in range(ATT_HEADS)],
                            axis=0) for h in range(ATT_HEADS)]


def _tile_mask(kind, start, off, tq, tk, n_pad, strict):
    if kind == "mid":
        return None
    row = lax.broadcasted_iota(jnp.int32, (tk, 1), 0)
    if kind == "first":
        return row >= n_pad
    dist = (lax.broadcasted_iota(jnp.int32, (tk, tq), 0)
            - lax.broadcasted_iota(jnp.int32, (tk, tq), 1))
    causal = (dist < off) if strict else (dist <= off)
    return causal & (row + start >= n_pad)


def _pipelined_tiles(n, produce, consume):
    produce(0, 0)

    @pl.when(n > 0)
    def _():
        produce(1, 1)
        consume(0, 0, "first")

    def body(k, carry):
        for parity in range(2):
            @pl.when(k % 2 == parity)
            def _():
                produce(1 - parity, k + 1)
                consume(parity, k, "mid")
        return carry

    lax.fori_loop(1, n, body, 0)
    for parity in range(2):
        @pl.when(n % 2 == parity)
        def _():
            consume(parity, n, "diag")


def _sb_kernel(k_ref, qt_ref, vt_ref, tri_ref, o_ref, acc_ref, carry_ref, *, tq, tk, blk, n_pad):
    i = pl.program_id(2)
    ratio = tk // tq
    c_diag = i // ratio
    off = (i - c_diag * ratio) * tq
    n_blocks = tk // blk
    q_cat = jnp.concatenate(_head_queries(qt_ref[...]), axis=1)
    tri = tri_ref[...]
    acc_ref[...] = jnp.zeros_like(acc_ref)
    carry_ref[...] = jnp.zeros_like(carry_ref)

    def step(c, kind):
        start = pl.multiple_of(c * tk, tk)
        ks = k_ref[pl.ds(start, tk), :]
        vt = vt_ref[:, pl.ds(start, tk)]
        mask = _tile_mask(kind, start, off, tq, tk, n_pad, strict=True)
        logits = _dot(ks, q_cat)
        for h in range(ATT_HEADS):
            x = logits[:, h * tq:(h + 1) * tq] * LOG2_E
            neg_abs = lax.bitcast_convert_type(
                lax.bitcast_convert_type(x, jnp.uint32) | jnp.uint32(0x80000000), F32)
            sp = jnp.maximum(x, 0.0) + jnp.log2(1.0 + jnp.exp2(neg_abs))
            log_beta = x - sp
            if mask is not None:
                sp = jnp.where(mask, sp, 0.0)
            hi = _trunc_bf16(sp)
            lo = (sp - hi).astype(BF16)
            hi = hi.astype(BF16)
            blocks = [slice(b * blk, (b + 1) * blk) for b in range(n_blocks)]
            within = _dot(tri, jnp.concatenate(
                [jnp.concatenate([hi[r], lo[r]], axis=0) for r in blocks], axis=1))
            run = carry_ref[h]
            w_blocks = [None] * n_blocks
            for b in reversed(range(n_blocks)):
                r = blocks[b]
                w_blocks[b] = jnp.exp2(log_beta[r] + within[:, b * tq:(b + 1) * tq] + run)
                run = run - jnp.sum(sp[r], axis=0, keepdims=True)
            carry_ref[h] = run
            w = jnp.concatenate(w_blocks, axis=0)
            if mask is not None:
                w = jnp.where(mask, w, 0.0)
            rows = _head_rows(h)
            acc_ref[rows, :] += _dot(vt[rows, :], w.astype(BF16))

    step(c_diag, "diag")

    def mid(n, carry):
        step(c_diag - n, "mid")
        return carry

    lax.fori_loop(1, c_diag, mid, 0)

    @pl.when(c_diag > 0)
    def _():
        step(0, "first")

    o_ref[...] = acc_ref[...].T.astype(o_ref.dtype)


def _fox_queries(qt, qb):
    tq = qt.shape[1]
    zeros = lambda n: jnp.zeros((n, tq), qt.dtype)
    out = []
    for h in range(ATT_HEADS):
        own = [qt[_head_rows(h)], zeros(HEAD_DIM)]
        pieces = (own if h % 2 == 0 else own[::-1]) + [
            zeros(h * BF16_ROWS), qb[h * BF16_ROWS:(h + 1) * BF16_ROWS],
            zeros(2 * HEAD_DIM - (h + 1) * BF16_ROWS)]
        out.append(jnp.concatenate([p for p in pieces if p.shape[0]], axis=0))
    return out


def _softmax_attn_kernel(*refs, tq, tk, n_pad, mode, out_scale):
    if mode == "fox":
        k_ref, kb_ref, qt_ref, qb_ref, vt_ref, o_ref, acc_ref, m_ref, sc_ref = refs
        q_heads = _fox_queries(qt_ref[...], qb_ref[...])
        q_cats = [jnp.concatenate(q_heads[2 * p:2 * p + 2], axis=1) for p in range(ATT_HEADS // 2)]
    else:
        k_ref, qt_ref, vt_ref, lam_ref, g_ref, o_ref, acc_ref, m_ref, sc_ref = refs
        q_cats = [jnp.concatenate(_head_queries(qt_ref[...]), axis=1)]
    i = pl.program_id(2)
    ratio = tk // tq
    c_diag = i // ratio
    off = (i - c_diag * ratio) * tq
    ones_rows = (lax.broadcasted_iota(jnp.int32, (BF16_ROWS, tk), 0) == 0).astype(BF16)
    v_rows = HEAD_DIM if mode == "fox" else 2 * HEAD_DIM
    acc_ref[...] = jnp.zeros_like(acc_ref)
    m_ref[...] = jnp.full_like(m_ref, NEG)

    def produce(slot, c):
        rows = pl.ds(pl.multiple_of(c * tk, tk), tk)
        ks = k_ref[rows, :]
        if mode == "fox":
            kb = kb_ref[rows, :]
            k_ops = [jnp.concatenate([ks[:, p * LANES:(p + 1) * LANES], kb], axis=1)
                     for p in range(ATT_HEADS // 2)]
        else:
            k_ops = [ks]
        width = ATT_HEADS * tq // len(q_cats)
        for n, (k_op, q) in enumerate(zip(k_ops, q_cats)):
            sc_ref[slot, :, n * width:(n + 1) * width] = _dot(k_op, q)

    def consume(slot, c, kind):
        start = pl.multiple_of(c * tk, tk)
        vt = vt_ref[:, pl.ds(start, tk)]
        mask = _tile_mask(kind, start, off, tq, tk, n_pad, strict=False)
        for h in range(ATT_HEADS):
            s = sc_ref[slot, :, h * tq:(h + 1) * tq]
            if mask is not None:
                s = jnp.where(mask, s, NEG)
            m_old = m_ref[h]
            m_new = jnp.maximum(m_old, jnp.max(s, axis=0, keepdims=True))
            alpha = jnp.exp(m_old - m_new)
            p = jnp.exp(s - m_new).astype(BF16)
            m_ref[h] = m_new
            vh = vt[_head_rows(h)] if mode == "fox" else vt[(h // 2) * v_rows:(h // 2 + 1) * v_rows]
            v_aug = jnp.concatenate([vh, ones_rows], axis=0)
            acc_ref[h] = alpha * acc_ref[h] + _dot(v_aug, p)

    _pipelined_tiles(c_diag, produce, consume)

    outs = []
    for h in range(ATT_HEADS):
        acc = acc_ref[h]
        outs.append(acc[:v_rows] * (1.0 / acc[v_rows:v_rows + 1]))
    if mode == "diff":
        heads = []
        for d in range(ATT_HEADS // 2):
            out_d = outs[2 * d] - lam_ref[0, 0] * outs[2 * d + 1]
            ms = jnp.mean(out_d * out_d, axis=0, keepdims=True)
            heads.append(out_d * lax.rsqrt(ms + RMS_EPS) * g_ref[...] * out_scale)
        outs = heads
    o_ref[...] = jnp.concatenate(outs, axis=0).T.astype(o_ref.dtype)


def _attention(mode, tok, tr, bsz, seq, k_blk, q_blk, v_blk, n_blk, extra=(), out_scale=1.0):
    tq, tk = ATT_TILE, ATT_KTILE
    nq = seq // tq
    k_spec = pl.BlockSpec((seq, ATT_BLOCK), lambda b, p, i: (b, k_blk + p))
    qt_spec = pl.BlockSpec((ATT_BLOCK, tq), lambda b, p, i: (q_blk + p, b * nq + i))
    vt_spec = pl.BlockSpec((ATT_BLOCK, seq), lambda b, p, i: (v_blk + p, b))
    m_scratch = pltpu.VMEM((ATT_HEADS, 1, tq), F32)
    sc_scratch = pltpu.VMEM((2, tk, ATT_HEADS * tq), F32)
    if mode == "sb":
        idx = jnp.arange(SB_BLOCK)
        tri = -(idx[None, :] > idx[:, None]).astype(BF16)
        args = [tok, tr, tr, jnp.concatenate([tri, tri], axis=1)]
        in_specs = [k_spec, qt_spec, vt_spec,
                    _resident((SB_BLOCK, 2 * SB_BLOCK), lambda b, p, i: (0, 0))]
        scratch = [pltpu.VMEM((ATT_BLOCK, tq), F32), m_scratch]
        body = functools.partial(_sb_kernel, tq=tq, tk=tk, blk=SB_BLOCK, n_pad=N_PAD)
    elif mode == "fox":
        k_bias, q_bias = extra
        args = [tok, k_bias, tr, q_bias, tr]
        in_specs = [k_spec, pl.BlockSpec((seq, LANES), lambda b, p, i: (b, p)), qt_spec,
                    pl.BlockSpec((None, None, ATT_HEADS * BF16_ROWS, tq), lambda b, p, i: (b, p, 0, i)),
                    vt_spec]
        scratch = [pltpu.VMEM((ATT_HEADS, HEAD_DIM + BF16_ROWS, tq), F32), m_scratch, sc_scratch]
        body = functools.partial(_softmax_attn_kernel, tq=tq, tk=tk, n_pad=N_PAD, mode=mode,
                                 out_scale=out_scale)
    else:
        lam, g_col = extra
        args = [tok, tr, tr, lam, g_col]
        in_specs = [k_spec, qt_spec, vt_spec, pl.BlockSpec(memory_space=pltpu.SMEM),
                    _resident((2 * HEAD_DIM, tq), lambda b, p, i: (0, 0))]
        scratch = [pltpu.VMEM((ATT_HEADS, 2 * HEAD_DIM + BF16_ROWS, tq), F32), m_scratch, sc_scratch]
        body = functools.partial(_softmax_attn_kernel, tq=tq, tk=tk, n_pad=N_PAD, mode=mode,
                                 out_scale=out_scale)
    return pl.pallas_call(
        body,
        out_shape=jax.ShapeDtypeStruct((bsz * seq, n_blk * ATT_BLOCK), BF16),
        grid=(bsz, n_blk, nq),
        in_specs=in_specs,
        out_specs=pl.BlockSpec((tq, ATT_BLOCK), lambda b, p, i: (b * nq + i, p)),
        scratch_shapes=scratch,
        compiler_params=_cparams(3),
        name=mode + "_attention",
    )(*args)


def _fcum_kernel(ft_ref, b_ref, tri_ref, o_ref, *, tile, n_pad):
    seq = ft_ref.shape[-1]
    tri = tri_ref[...]
    carry = jnp.zeros((ft_ref.shape[0], 1), F32)
    for c in range(seq // tile):
        x = ft_ref[:, c * tile:(c + 1) * tile] + b_ref[...]
        log_f = jnp.minimum(x, 0.0) - jnp.log(1.0 + jnp.exp(-jnp.abs(x)))
        pos = lax.broadcasted_iota(jnp.int32, (1, tile), 1) + c * tile
        log_f = jnp.where(pos >= n_pad, log_f, 0.0)
        hi = log_f.astype(BF16)
        r1 = log_f - hi.astype(F32)
        mid = r1.astype(BF16)
        lo = (r1 - mid.astype(F32)).astype(BF16)
        cs = _dot(jnp.concatenate([hi, mid, lo], axis=1), tri)
        o_ref[:, c * tile:(c + 1) * tile] = cs + carry
        carry = carry + jnp.sum(log_f, axis=1, keepdims=True)


def _forget_cumsum(ft, b_f):
    bsz, rows, seq = ft.shape
    tile = ATT_TILE
    idx = jnp.arange(tile)
    tri = (idx[:, None] <= idx[None, :]).astype(BF16)
    tri3 = jnp.concatenate([tri, tri, tri], axis=0)
    return pl.pallas_call(
        functools.partial(_fcum_kernel, tile=tile, n_pad=N_PAD),
        out_shape=jax.ShapeDtypeStruct((bsz, rows, seq), F32),
        grid=(bsz,),
        in_specs=[pl.BlockSpec((None, rows, seq), lambda b: (b, 0, 0)),
                  _resident((rows, 1), lambda b: (0, 0)),
                  _resident((3 * tile, tile), lambda b: (0, 0))],
        out_specs=pl.BlockSpec((None, rows, seq), lambda b: (b, 0, 0)),
        compiler_params=_cparams(1),
        name="forget_cumsum",
    )(ft, b_f, tri3)


def _forget_bias_operands(f_row, n_blk):
    bsz, heads, seq = f_row.shape
    hi = _trunc_bf16(f_row)
    mid = _trunc_bf16(f_row - hi)
    lo = ((f_row - hi) - mid).astype(BF16)
    hi, mid = hi.astype(BF16), mid.astype(BF16)
    one = jnp.ones_like(hi)
    zero = jnp.zeros((bsz, heads, BF16_ROWS - 6, seq), BF16)
    k_terms = jnp.concatenate([jnp.stack([hi, mid, lo, one, one, one], axis=2), zero], axis=2)
    q_terms = jnp.concatenate([jnp.stack([-one, -one, -one, hi, mid, lo], axis=2), zero], axis=2)
    used = ATT_HEADS * BF16_ROWS
    k_bias = jnp.transpose(k_terms.reshape(bsz, n_blk, used, seq), (0, 3, 1, 2))
    k_bias = jnp.concatenate([k_bias, jnp.zeros((bsz, seq, n_blk, LANES - used), BF16)], axis=3)
    q_bias = q_terms.reshape(bsz, n_blk, used, seq)
    return k_bias.reshape(bsz * seq, n_blk * LANES), q_bias


def _s5_kernel(u_ref, toep_h, toep_l, sb_h, sb_l, at_ref, ca_h, ca_l, y_ref, acc_ref, st_ref,
               *, t, row_chunk):
    n_chunks = st_ref.shape[0]
    half = st_ref.shape[1] // 2

    def split(x):
        hi = x.astype(BF16)
        return hi, (x - hi.astype(F32)).astype(BF16)

    def chunk_rows(ref, r, step):
        return ref.at[pl.ds(r * row_chunk * t + step, row_chunk, stride=t), :]

    def stage1(r, c):
        ut = jnp.concatenate([chunk_rows(u_ref, r, step)[...] for step in range(t)], axis=1)
        uh, ul = split(ut)
        sl = pl.ds(pl.multiple_of(r * row_chunk, 8), row_chunk)
        acc_ref[sl, :] = _dot3(uh, ul, toep_h[...], toep_l[...])
        st_ref[sl, :] = _dot3(uh, ul, sb_h[...], sb_l[...])
        return c

    lax.fori_loop(0, n_chunks // row_chunk, stage1, 0)

    a_re = at_ref[0:1, :]
    a_im = at_ref[1:2, :]

    def scan(c, state):
        x_re, x_im = state
        row = pl.ds(c, 1)
        s_re = st_ref[row, :half]
        s_im = st_ref[row, half:]
        st_ref[row, :half] = x_re
        st_ref[row, half:] = x_im
        return (a_re * x_re - a_im * x_im + s_re, a_re * x_im + a_im * x_re + s_im)

    zero = jnp.zeros((1, half), F32)
    lax.fori_loop(0, n_chunks, scan, (zero, zero), unroll=4)

    def stage3(r, c):
        sl = pl.ds(pl.multiple_of(r * row_chunk, 8), row_chunk)
        xh, xl = split(st_ref[sl, :])
        y = acc_ref[sl, :] + _dot3(xh, xl, ca_h[...], ca_l[...])
        y = 0.5 * y * (1.0 + jnp.tanh(math.sqrt(2.0 / math.pi) * (y + 0.044715 * (y * y * y))))
        for step in range(t):
            chunk_rows(y_ref, r, step)[...] = y[:, step * LANES:(step + 1) * LANES]
        return c

    lax.fori_loop(0, n_chunks // row_chunk, stage3, 0)


def _s5_operators(lam_re, lam_im, log_dt, b_re, b_im, c_re, c_im, d_skip):
    t = S5_T
    lre, lim = lam_re.astype(F32), lam_im.astype(F32)
    dt = jnp.exp(log_dt.astype(F32))[:, None]
    mag = jnp.exp(lre * dt)
    a_re, a_im = mag * jnp.cos(lim * dt), mag * jnp.sin(lim * dt)
    den = lre * lre + lim * lim
    g_re = ((a_re - 1.0) * lre + a_im * lim) / den
    g_im = (a_im * lre - (a_re - 1.0) * lim) / den
    br, bi = b_re.astype(F32), b_im.astype(F32)
    bb_re = g_re[..., None] * br - g_im[..., None] * bi
    bb_im = g_re[..., None] * bi + g_im[..., None] * br
    k = jnp.arange(t + 1, dtype=F32)[:, None, None]
    pw_mag = jnp.exp(k * (lre * dt)[None])
    pw_re = pw_mag * jnp.cos(k * (lim * dt)[None])
    pw_im = pw_mag * jnp.sin(k * (lim * dt)[None])
    cr, ci = c_re.astype(F32), c_im.astype(F32)
    hp = lax.Precision.HIGHEST
    ab_re = pw_re[..., None] * bb_re[None] - pw_im[..., None] * bb_im[None]
    ab_im = pw_re[..., None] * bb_im[None] + pw_im[..., None] * bb_re[None]
    kern = (jnp.einsum('ghp,kgpe->kghe', cr, ab_re, precision=hp)
            - jnp.einsum('ghp,kgpe->kghe', ci, ab_im, precision=hp))
    lag = jnp.arange(t)[None, :] - jnp.arange(t)[:, None]
    toep = jnp.where((lag >= 0)[:, :, None, None, None],
                     kern[jnp.clip(lag, 0, t)], 0.0)
    eye_t = jnp.eye(t, dtype=F32)
    eye_h = jnp.eye(S5_GROUP, dtype=F32)
    toep = toep + (eye_t[:, :, None, None, None] * eye_h[None, None, None]
                   * d_skip.astype(F32)[None, None, :, :, None])
    g_n = toep.shape[2]
    nb, gb = g_n // S5_BLOCK_GROUPS, S5_BLOCK_GROUPS
    eye_g = jnp.eye(gb, dtype=F32)
    width = t * gb * S5_GROUP
    toep = jnp.transpose(toep, (2, 0, 4, 1, 3)).reshape(nb, gb, t, S5_GROUP, t, S5_GROUP)
    toep = jnp.einsum('bgjaih,gk->bjgaikh', toep, eye_g).reshape(nb, width, width)
    rev = t - 1 - jnp.arange(t)

    def state_in(ab):
        x = jnp.transpose(ab[rev], (1, 0, 3, 2)).reshape(nb, gb, t, S5_GROUP, S5_STATE)
        return jnp.einsum('bgjap,gk->bjgakp', x, eye_g).reshape(nb, width, gb * S5_STATE)

    sb = jnp.concatenate([state_in(ab_re), state_in(ab_im)], axis=2)
    e_re, e_im = pw_re[1:], pw_im[1:]
    ca_re = cr[None] * e_re[:, :, None, :] - ci[None] * e_im[:, :, None, :]
    ca_im = -(cr[None] * e_im[:, :, None, :] + ci[None] * e_re[:, :, None, :])

    def state_out(ca):
        x = jnp.transpose(ca, (1, 3, 0, 2)).reshape(nb, gb, S5_STATE, t, S5_GROUP)
        return jnp.einsum('bgpih,gk->bgpikh', x, eye_g).reshape(nb, gb * S5_STATE, width)

    ca = jnp.concatenate([state_out(ca_re), state_out(ca_im)], axis=1)
    at = jnp.stack([pw_re[t].reshape(nb, gb * S5_STATE), pw_im[t].reshape(nb, gb * S5_STATE)], axis=1)
    return [*_split_bf16(toep), *_split_bf16(sb), at, *_split_bf16(ca)]


def _s5_layer(u, ops):
    bsz, seq, ch = u.shape
    t = S5_T
    n_chunks = seq // t
    width = t * LANES
    n_state = 2 * S5_BLOCK_GROUPS * S5_STATE
    row_chunk = next(n_chunks // n for n in (8, 4, 2, 1) if n_chunks % (8 * n) == 0)
    op_spec = lambda r, c: pl.BlockSpec((None, r, c), lambda p, b: (p, 0, 0),
                                        pipeline_mode=pl.Buffered(1))
    slab = pl.BlockSpec((None, seq, LANES), lambda p, b: (b, 0, p))
    return pl.pallas_call(
        functools.partial(_s5_kernel, t=t, row_chunk=row_chunk),
        out_shape=jax.ShapeDtypeStruct((bsz, seq, ch), F32),
        grid=(ch // LANES, bsz),
        in_specs=[slab, op_spec(width, width), op_spec(width, width),
                  op_spec(width, n_state), op_spec(width, n_state), op_spec(2, n_state // 2),
                  op_spec(n_state, width), op_spec(n_state, width)],
        out_specs=slab,
        scratch_shapes=[pltpu.VMEM((n_chunks, width), F32), pltpu.VMEM((n_chunks, n_state), F32)],
        compiler_params=_cparams(2),
        name="s5_chunked",
    )(u, *ops)


def _post_kernel(*refs, n_pad, ff_chunk, glu, final):
    h_ref, a_ref, b_ref = refs[:3]
    pos = 3
    if glu:
        wglu_ref, bglu_ref = refs[pos:pos + 2]
        pos += 2
    wo_ref, gf_ref, wgu_ref, wd_ref = refs[pos:pos + 4]
    pos += 4
    if final:
        gl_ref = refs[pos]
        pos += 1
    o_ref, acc_ref = refs[pos:pos + 2]

    tm = h_ref.shape[0]
    half = a_ref.shape[-1]
    d_ff = wd_ref.shape[0]
    row = lax.broadcasted_iota(jnp.int32, (tm, 1), 0) + pl.program_id(1) * tm
    valid = row >= n_pad

    a = a_ref[...]
    if glu:
        y = b_ref[...]
        z = _dot(y.astype(BF16), wglu_ref[...]) + bglu_ref[...]
        b = (y * (1.0 / (1.0 + jnp.exp(-z)))).astype(BF16)
    else:
        b = b_ref[...]
    mix = _dot(a, wo_ref[:half, :]) + _dot(b, wo_ref[half:, :])
    h1 = jnp.where(valid, h_ref[...] + mix, 0.0)
    hn = _rms(h1, gf_ref[...]).astype(BF16)
    acc_ref[...] = jnp.zeros_like(acc_ref)
    for c in range(0, d_ff, ff_chunk):
        gate = _dot(hn, wgu_ref[:, c:c + ff_chunk])
        up = _dot(hn, wgu_ref[:, d_ff + c:d_ff + c + ff_chunk])
        act = (gate * (1.0 / (1.0 + jnp.exp(-gate))) * up).astype(BF16)
        acc_ref[...] += _dot(act, wd_ref[c:c + ff_chunk, :])
    h2 = jnp.where(valid, h1 + acc_ref[...], 0.0)
    if final:
        h2 = _rms(h2, gl_ref[...])
    o_ref[...] = h2


def _post_mixer(h, a, b, w_out, g_ffn, w_gu, w_down, glu=None, final_g=None):
    bsz, seq, d = h.shape
    tm = next(t for t in (768, 512, 256) if seq % t == 0)
    half = a.shape[-1]
    d_ff = w_down.shape[0]
    row_spec = lambda w: pl.BlockSpec((None, tm, w), lambda bi, i: (bi, i, 0))
    const = lambda shape: _resident(shape, lambda bi, i: (0,) * len(shape))
    args = [h, a, b]
    in_specs = [row_spec(d), row_spec(half), row_spec(half)]
    if glu is not None:
        args += list(glu)
        in_specs += [const((half, half)), const((1, half))]
    args += [w_out, g_ffn, w_gu, w_down]
    in_specs += [const((2 * half, d)), const((1, d)), const((d, 2 * d_ff)), const((d_ff, d))]
    if final_g is not None:
        args.append(final_g)
        in_specs.append(const((1, d)))
    return pl.pallas_call(
        functools.partial(_post_kernel, n_pad=N_PAD, ff_chunk=256,
                          glu=glu is not None, final=final_g is not None),
        out_shape=jax.ShapeDtypeStruct((bsz, seq, d), F32),
        grid=(bsz, seq // tm),
        in_specs=in_specs,
        out_specs=row_spec(d),
        scratch_shapes=[pltpu.VMEM((tm, d), F32)],
        compiler_params=_cparams(2),
        name="outproj_ffn",
    )(*args)


def kernel(x, meta_tokens, norm_mix_g, norm_ffn_g, final_norm_g, even_w_in, even_w_out,
           diff_lam_q1, diff_lam_k1, diff_lam_q2, diff_lam_k2, diff_subln_g, odd_w_in,
           odd_w_out, fox_b_f, s5_lam_re, s5_lam_im, s5_log_dt, s5_b_re, s5_b_im, s5_c_re,
           s5_c_im, s5_d, s5_w_glu, s5_b_glu, ffn_w_gate_up, ffn_w_down):
    bsz, n_seq, d = x.shape
    seq = n_seq + FRONT
    half = d // 2
    n_blk = half // ATT_BLOCK
    scale = HEAD_DIM ** -0.5
    fox_heads = half // HEAD_DIM

    pad = jnp.zeros((bsz, N_PAD, d), x.dtype)
    meta = jnp.broadcast_to(meta_tokens[None].astype(x.dtype), (bsz, N_META, d))
    h = jnp.concatenate([pad, meta, x], axis=1)

    w_in = even_w_in[0]
    col = lambda n: w_in[:, n * half:(n + 1) * half]
    w_tok = jnp.concatenate([col(1), col(4)], axis=1).astype(BF16)
    w_tr = jnp.concatenate([col(0) * scale, col(2), col(3) * scale, col(5)], axis=1).T.astype(BF16)
    tok, tr = _norm_proj(h.reshape(bsz * seq, d), norm_mix_g[0][None, :], w_tok, w_tr)
    sb = _attention("sb", tok, tr, bsz, seq, 0, 0, n_blk, n_blk)
    lam_init = 0.8 - 0.6 * math.exp(-0.3 * 0)
    lam = (jnp.exp(jnp.sum(diff_lam_q1[0].astype(F32) * diff_lam_k1[0].astype(F32)))
           - jnp.exp(jnp.sum(diff_lam_q2[0].astype(F32) * diff_lam_k2[0].astype(F32))) + lam_init)
    g_col = jnp.broadcast_to(diff_subln_g[0].astype(F32)[:, None], (2 * HEAD_DIM, ATT_TILE))
    df = _attention("diff", tok, tr, bsz, seq, n_blk, 2 * n_blk, 3 * n_blk, n_blk,
                    (lam.reshape(1, 1).astype(F32), g_col), out_scale=1.0 - lam_init)
    h = _post_mixer(h, sb.reshape(bsz, seq, half), df.reshape(bsz, seq, half),
                    even_w_out[0].astype(BF16), norm_ffn_g[0][None, :],
                    ffn_w_gate_up[0].astype(BF16), ffn_w_down[0].astype(BF16))

    w_in = odd_w_in[0]
    col = lambda n: w_in[:, n * half:(n + 1) * half]
    w_tok = col(1).astype(BF16)
    w_tr = jnp.concatenate([col(0) * scale, col(2)], axis=1).T.astype(BF16)
    w_f = w_in[:, 3 * half:3 * half + fox_heads]
    w_u = w_in[:, 3 * half + fox_heads:].astype(BF16)
    f_rows = BF16_ROWS
    w_ft = jnp.zeros((f_rows, d), F32).at[:fox_heads].set(w_f.T)
    tok, tr, u, ft = _norm_proj(h.reshape(bsz * seq, d), norm_mix_g[1][None, :], w_tok, w_tr,
                                odd_weights=(w_u,) + _split_bf16(w_ft))
    ft = jnp.transpose(ft.reshape(f_rows, bsz, seq), (1, 0, 2))
    b_f = jnp.zeros((f_rows, 1), F32).at[:fox_heads, 0].set(fox_b_f[0].astype(F32))
    f_row = _forget_cumsum(ft, b_f)[:, :fox_heads]
    fox = _attention("fox", tok, tr, bsz, seq, 0, 0, n_blk, n_blk,
                     _forget_bias_operands(f_row, n_blk))
    ops = _s5_operators(s5_lam_re[0], s5_lam_im[0], s5_log_dt[0], s5_b_re[0], s5_b_im[0],
                        s5_c_re[0], s5_c_im[0], s5_d[0])
    ssm = _s5_layer(u.reshape(bsz, seq, -1), ops)
    h = _post_mixer(h, fox.reshape(bsz, seq, half), ssm, odd_w_out[0].astype(BF16),
                    norm_ffn_g[1][None, :], ffn_w_gate_up[1].astype(BF16),
                    ffn_w_down[1].astype(BF16),
                    glu=(s5_w_glu[0].astype(BF16), s5_b_glu[0][None, :].astype(F32)),
                    final_g=final_norm_g[None, :])
    return h[:, FRONT:]
```

```python
import functools
import math

import jax
import jax.numpy as jnp
from jax import lax
from jax.experimental import pallas as pl
from jax.experimental.pallas import tpu as pltpu

F32 = jnp.float32
BF16 = jnp.bfloat16

HEAD_DIM = 64
LANES = 128
ATT_BLOCK = 256
ATT_HEADS = ATT_BLOCK // HEAD_DIM
BF16_ROWS = 16
N_META = 16
ATT_TILE = 256
ATT_KTILE = 768
SB_BLOCK = 128
FRONT = ATT_TILE
N_PAD = FRONT - N_META
NEG = -1e30
RMS_EPS = 1e-6
S5_GROUP = 16
S5_STATE = 64
S5_T = 8
S5_BLOCK_GROUPS = LANES // S5_GROUP
VMEM_LIMIT = 56 * 1024 * 1024


def _cparams(n_axes):
    return pltpu.CompilerParams(
        dimension_semantics=("arbitrary",) * n_axes, vmem_limit_bytes=VMEM_LIMIT)


def _resident(shape, index_map):
    return pl.BlockSpec(shape, index_map, pipeline_mode=pl.Buffered(1))


def _trunc_bf16(x):
    bits = lax.bitcast_convert_type(x, jnp.uint32) & jnp.uint32(0xFFFF0000)
    return lax.bitcast_convert_type(bits, F32)


def _split_bf16(w):
    hi = _trunc_bf16(w)
    return hi.astype(BF16), (w - hi).astype(BF16)


def _dot(a, b):
    return jnp.dot(a, b, preferred_element_type=F32)


def _dot_nt(a, b):
    return lax.dot_general(a, b, (((1,), (1,)), ((), ())), preferred_element_type=F32)


def _dot3(a_hi, a_lo, b_hi, b_lo):
    return _dot(a_hi, b_hi) + _dot(a_lo, b_hi) + _dot(a_hi, b_lo)


def _rms(x, g):
    ms = jnp.mean(x * x, axis=-1, keepdims=True)
    return x * lax.rsqrt(ms + RMS_EPS) * g


def _proj_kernel(*refs, n_chunk, odd):
    if odd:
        (x_ref, g_ref, wtok_ref, wtr_ref, wu_ref, wfh_ref, wfl_ref,
         tok_ref, tr_ref, u_ref, ft_ref) = refs
    else:
        x_ref, g_ref, wtok_ref, wtr_ref, tok_ref, tr_ref = refs
    hn32 = _rms(x_ref[...], g_ref[...])
    hn = hn32.astype(BF16)
    for c in range(0, tok_ref.shape[-1], n_chunk):
        tok_ref[:, c:c + n_chunk] = _dot(hn, wtok_ref[:, c:c + n_chunk]).astype(tok_ref.dtype)
    for c in range(0, tr_ref.shape[0], n_chunk):
        tr_ref[c:c + n_chunk, :] = _dot_nt(wtr_ref[c:c + n_chunk, :], hn).astype(tr_ref.dtype)
    if odd:
        u_ref[...] = _dot(hn, wu_ref[...])
        hn_lo = (hn32 - hn.astype(F32)).astype(BF16)
        wfh = wfh_ref[...]
        ft_ref[...] = _dot_nt(wfh, hn) + _dot_nt(wfh, hn_lo) + _dot_nt(wfl_ref[...], hn)


def _norm_proj(h, g, w_tok, w_tr, odd_weights=None):
    m, d = h.shape
    tm = next(t for t in (512, 256) if m % t == 0)
    n_tok, n_tr = w_tok.shape[1], w_tr.shape[0]
    const = lambda shape: _resident(shape, lambda i: (0, 0))
    args = [h, g, w_tok, w_tr]
    in_specs = [pl.BlockSpec((tm, d), lambda i: (i, 0)), const((1, d)),
                const((d, n_tok)), const((n_tr, d))]
    out_shape = [jax.ShapeDtypeStruct((m, n_tok), BF16), jax.ShapeDtypeStruct((n_tr, m), BF16)]
    out_specs = [pl.BlockSpec((tm, n_tok), lambda i: (i, 0)),
                 pl.BlockSpec((n_tr, tm), lambda i: (0, i))]
    if odd_weights is not None:
        wu, wfh, wfl = odd_weights
        nu, nf = wu.shape[1], wfh.shape[0]
        args += [wu, wfh, wfl]
        in_specs += [const((d, nu)), const((nf, d)), const((nf, d))]
        out_shape += [jax.ShapeDtypeStruct((m, nu), F32), jax.ShapeDtypeStruct((nf, m), F32)]
        out_specs += [pl.BlockSpec((tm, nu), lambda i: (i, 0)),
                      pl.BlockSpec((nf, tm), lambda i: (0, i))]
    return pl.pallas_call(
        functools.partial(_proj_kernel, n_chunk=256, odd=odd_weights is not None),
        out_shape=tuple(out_shape),
        grid=(m // tm,),
        in_specs=in_specs,
        out_specs=tuple(out_specs),
        compiler_params=_cparams(1),
        name="norm_proj_odd" if odd_weights is not None else "norm_proj_even",
    )(*args)


def _head_rows(h):
    return slice(h * HEAD_DIM, (h + 1) * HEAD_DIM)


def _head_queries(qt):
    zero = jnp.zeros((HEAD_DIM, qt.shape[1]), qt.dtype)
    return [jnp.concatenate([qt[_head_rows(r)] if r == h else zero for r in range(ATT_HEADS)],
                            axis=0) for h in range(ATT_HEADS)]


def _tile_mask(kind, start, off, tq, tk, n_pad, strict):
    if kind == "mid":
        return None
    row = lax.broadcasted_iota(jnp.int32, (tk, 1), 0)
    if kind == "first":
        return row >= n_pad
    dist = (lax.broadcasted_iota(jnp.int32, (tk, tq), 0)
            - lax.broadcasted_iota(jnp.int32, (tk, tq), 1))
    causal = (dist < off) if strict else (dist <= off)
    return causal & (row + start >= n_pad)


def _pipelined_tiles(n, produce, consume):
    produce(0, 0)

    @pl.when(n > 0)
    def _():
        produce(1, 1)
        consume(0, 0, "first")

    def body(k, carry):
        for parity in range(2):
            @pl.when(k % 2 == parity)
            def _():
                produce(1 - parity, k + 1)
                consume(parity, k, "mid")
        return carry

    lax.fori_loop(1, n, body, 0)
    for parity in range(2):
        @pl.when(n % 2 == parity)
        def _():
            consume(parity, n, "diag")


def _sb_kernel(k_ref, qt_ref, vt_ref, tri_ref, o_ref, acc_ref, carry_ref, *, tq, tk, blk, n_pad):
    i = pl.program_id(2)
    ratio = tk // tq
    c_diag = i // ratio
    off = (i - c_diag * ratio) * tq
    n_blocks = tk // blk
    q_cat = jnp.concatenate(_head_queries(qt_ref[...]), axis=1)
    tri = tri_ref[...]
    acc_ref[...] = jnp.zeros_like(acc_ref)
    carry_ref[...] = jnp.zeros_like(carry_ref)

    def step(c, kind):
        start = pl.multiple_of(c * tk, tk)
        ks = k_ref[pl.ds(start, tk), :]
        vt = vt_ref[:, pl.ds(start, tk)]
        mask = _tile_mask(kind, start, off, tq, tk, n_pad, strict=True)
        logits = _dot(ks, q_cat)
        for h in range(ATT_HEADS):
            x = logits[:, h * tq:(h + 1) * tq]
            sp = jnp.maximum(x, 0.0) + jnp.log(1.0 + jnp.exp(-jnp.abs(x)))
            log_beta = x - sp
            if mask is not None:
                sp = jnp.where(mask, sp, 0.0)
            hi = sp.astype(BF16)
            lo = (sp - hi.astype(F32)).astype(BF16)
            blocks = [slice(b * blk, (b + 1) * blk) for b in range(n_blocks)]
            within = _dot(tri, jnp.concatenate(
                [jnp.concatenate([hi[r], lo[r]], axis=0) for r in blocks], axis=1))
            run = carry_ref[h]
            w_blocks = [None] * n_blocks
            for b in reversed(range(n_blocks)):
                r = blocks[b]
                w_blocks[b] = jnp.exp(log_beta[r] + within[:, b * tq:(b + 1) * tq] + run)
                run = run - jnp.sum(sp[r], axis=0, keepdims=True)
            carry_ref[h] = run
            w = jnp.concatenate(w_blocks, axis=0)
            if mask is not None:
                w = jnp.where(mask, w, 0.0)
            rows = _head_rows(h)
            acc_ref[rows, :] += _dot(vt[rows, :], w.astype(BF16))

    step(c_diag, "diag")

    def mid(n, carry):
        step(c_diag - n, "mid")
        return carry

    lax.fori_loop(1, c_diag, mid, 0)

    @pl.when(c_diag > 0)
    def _():
        step(0, "first")

    o_ref[...] = acc_ref[...].T.astype(o_ref.dtype)


def _fox_queries(qt, qb):
    tq = qt.shape[1]
    zeros = lambda n: jnp.zeros((n, tq), qt.dtype)
    out = []
    for h in range(ATT_HEADS):
        own = [qt[_head_rows(h)], zeros(HEAD_DIM)]
        pieces = (own if h % 2 == 0 else own[::-1]) + [
            zeros(h * BF16_ROWS), qb[h * BF16_ROWS:(h + 1) * BF16_ROWS],
            zeros(2 * HEAD_DIM - (h + 1) * BF16_ROWS)]
        out.append(jnp.concatenate([p for p in pieces if p.shape[0]], axis=0))
    return out


def _softmax_attn_kernel(*refs, tq, tk, n_pad, mode, out_scale):
    if mode == "fox":
        k_ref, kb_ref, qt_ref, qb_ref, vt_ref, o_ref, acc_ref, m_ref, sc_ref = refs
        q_heads = _fox_queries(qt_ref[...], qb_ref[...])
        q_cats = [jnp.concatenate(q_heads[2 * p:2 * p + 2], axis=1) for p in range(ATT_HEADS // 2)]
    else:
        k_ref, qt_ref, vt_ref, lam_ref, g_ref, o_ref, acc_ref, m_ref, sc_ref = refs
        q_cats = [jnp.concatenate(_head_queries(qt_ref[...]), axis=1)]
    i = pl.program_id(2)
    ratio = tk // tq
    c_diag = i // ratio
    off = (i - c_diag * ratio) * tq
    ones_rows = (lax.broadcasted_iota(jnp.int32, (BF16_ROWS, tk), 0) == 0).astype(BF16)
    v_rows = HEAD_DIM if mode == "fox" else 2 * HEAD_DIM
    acc_ref[...] = jnp.zeros_like(acc_ref)
    m_ref[...] = jnp.full_like(m_ref, NEG)

    def produce(slot, c):
        rows = pl.ds(pl.multiple_of(c * tk, tk), tk)
        ks = k_ref[rows, :]
        if mode == "fox":
            kb = kb_ref[rows, :]
            k_ops = [jnp.concatenate([ks[:, p * LANES:(p + 1) * LANES], kb], axis=1)
                     for p in range(ATT_HEADS // 2)]
        else:
            k_ops = [ks]
        width = ATT_HEADS * tq // len(q_cats)
        for n, (k_op, q) in enumerate(zip(k_ops, q_cats)):
            sc_ref[slot, :, n * width:(n + 1) * width] = _dot(k_op, q)

    def consume(slot, c, kind):
        start = pl.multiple_of(c * tk, tk)
        vt = vt_ref[:, pl.ds(start, tk)]
        mask = _tile_mask(kind, start, off, tq, tk, n_pad, strict=False)
        for h in range(ATT_HEADS):
            s = sc_ref[slot, :, h * tq:(h + 1) * tq]
            if mask is not None:
                s = jnp.where(mask, s, NEG)
            m_old = m_ref[h]
            m_new = jnp.maximum(m_old, jnp.max(s, axis=0, keepdims=True))
            alpha = jnp.exp(m_old - m_new)
            p = jnp.exp(s - m_new).astype(BF16)
            m_ref[h] = m_new
            vh = vt[_head_rows(h)] if mode == "fox" else vt[(h // 2) * v_rows:(h // 2 + 1) * v_rows]
            v_aug = jnp.concatenate([vh, ones_rows], axis=0)
            acc_ref[h] = alpha * acc_ref[h] + _dot(v_aug, p)

    _pipelined_tiles(c_diag, produce, consume)

    outs = []
    for h in range(ATT_HEADS):
        acc = acc_ref[h]
        outs.append(acc[:v_rows] * (1.0 / acc[v_rows:v_rows + 1]))
    if mode == "diff":
        heads = []
        for d in range(ATT_HEADS // 2):
            out_d = outs[2 * d] - lam_ref[0, 0] * outs[2 * d + 1]
            ms = jnp.mean(out_d * out_d, axis=0, keepdims=True)
            heads.append(out_d * lax.rsqrt(ms + RMS_EPS) * g_ref[...] * out_scale)
        outs = heads
    o_ref[...] = jnp.concatenate(outs, axis=0).T.astype(o_ref.dtype)


def _attention(mode, tok, tr, bsz, seq, k_blk, q_blk, v_blk, n_blk, extra=(), out_scale=1.0):
    tq, tk = ATT_TILE, ATT_KTILE
    nq = seq // tq
    k_spec = pl.BlockSpec((seq, ATT_BLOCK), lambda b, p, i: (b, k_blk + p))
    qt_spec = pl.BlockSpec((ATT_BLOCK, tq), lambda b, p, i: (q_blk + p, b * nq + i))
    vt_spec = pl.BlockSpec((ATT_BLOCK, seq), lambda b, p, i: (v_blk + p, b))
    m_scratch = pltpu.VMEM((ATT_HEADS, 1, tq), F32)
    sc_scratch = pltpu.VMEM((2, tk, ATT_HEADS * tq), F32)
    if mode == "sb":
        idx = jnp.arange(SB_BLOCK)
        tri = -(idx[None, :] > idx[:, None]).astype(BF16)
        args = [tok, tr, tr, jnp.concatenate([tri, tri], axis=1)]
        in_specs = [k_spec, qt_spec, vt_spec,
                    _resident((SB_BLOCK, 2 * SB_BLOCK), lambda b, p, i: (0, 0))]
        scratch = [pltpu.VMEM((ATT_BLOCK, tq), F32), m_scratch]
        body = functools.partial(_sb_kernel, tq=tq, tk=tk, blk=SB_BLOCK, n_pad=N_PAD)
    elif mode == "fox":
        k_bias, q_bias = extra
        args = [tok, k_bias, tr, q_bias, tr]
        in_specs = [k_spec, pl.BlockSpec((seq, LANES), lambda b, p, i: (b, p)), qt_spec,
                    pl.BlockSpec((None, None, ATT_HEADS * BF16_ROWS, tq), lambda b, p, i: (b, p, 0, i)),
                    vt_spec]
        scratch = [pltpu.VMEM((ATT_HEADS, HEAD_DIM + BF16_ROWS, tq), F32), m_scratch, sc_scratch]
        body = functools.partial(_softmax_attn_kernel, tq=tq, tk=tk, n_pad=N_PAD, mode=mode,
                                 out_scale=out_scale)
    else:
        lam, g_col = extra
        args = [tok, tr, tr, lam, g_col]
        in_specs = [k_spec, qt_spec, vt_spec, pl.BlockSpec(memory_space=pltpu.SMEM),
                    _resident((2 * HEAD_DIM, tq), lambda b, p, i: (0, 0))]
        scratch = [pltpu.VMEM((ATT_HEADS, 2 * HEAD_DIM + BF16_ROWS, tq), F32), m_scratch, sc_scratch]
        body = functools.partial(_softmax_attn_kernel, tq=tq, tk=tk, n_pad=N_PAD, mode=mode,
                                 out_scale=out_scale)
    return pl.pallas_call(
        body,
        out_shape=jax.ShapeDtypeStruct((bsz * seq, n_blk * ATT_BLOCK), BF16),
        grid=(bsz, n_blk, nq),
        in_specs=in_specs,
        out_specs=pl.BlockSpec((tq, ATT_BLOCK), lambda b, p, i: (b * nq + i, p)),
        scratch_shapes=scratch,
        compiler_params=_cparams(3),
        name=mode + "_attention",
    )(*args)


def _fcum_kernel(ft_ref, b_ref, tri_ref, o_ref, *, tile, n_pad):
    seq = ft_ref.shape[-1]
    tri = tri_ref[...]
    carry = jnp.zeros((ft_ref.shape[0], 1), F32)
    for c in range(seq // tile):
        x = ft_ref[:, c * tile:(c + 1) * tile] + b_ref[...]
        log_f = jnp.minimum(x, 0.0) - jnp.log(1.0 + jnp.exp(-jnp.abs(x)))
        pos = lax.broadcasted_iota(jnp.int32, (1, tile), 1) + c * tile
        log_f = jnp.where(pos >= n_pad, log_f, 0.0)
        hi = log_f.astype(BF16)
        r1 = log_f - hi.astype(F32)
        mid = r1.astype(BF16)
        lo = (r1 - mid.astype(F32)).astype(BF16)
        cs = _dot(jnp.concatenate([hi, mid, lo], axis=1), tri)
        o_ref[:, c * tile:(c + 1) * tile] = cs + carry
        carry = carry + jnp.sum(log_f, axis=1, keepdims=True)


def _forget_cumsum(ft, b_f):
    bsz, rows, seq = ft.shape
    tile = ATT_TILE
    idx = jnp.arange(tile)
    tri = (idx[:, None] <= idx[None, :]).astype(BF16)
    tri3 = jnp.concatenate([tri, tri, tri], axis=0)
    return pl.pallas_call(
        functools.partial(_fcum_kernel, tile=tile, n_pad=N_PAD),
        out_shape=jax.ShapeDtypeStruct((bsz, rows, seq), F32),
        grid=(bsz,),
        in_specs=[pl.BlockSpec((None, rows, seq), lambda b: (b, 0, 0)),
                  _resident((rows, 1), lambda b: (0, 0)),
                  _resident((3 * tile, tile), lambda b: (0, 0))],
        out_specs=pl.BlockSpec((None, rows, seq), lambda b: (b, 0, 0)),
        compiler_params=_cparams(1),
        name="forget_cumsum",
    )(ft, b_f, tri3)


def _forget_bias_operands(f_row, n_blk):
    bsz, heads, seq = f_row.shape
    hi = _trunc_bf16(f_row)
    mid = _trunc_bf16(f_row - hi)
    lo = ((f_row - hi) - mid).astype(BF16)
    hi, mid = hi.astype(BF16), mid.astype(BF16)
    one = jnp.ones_like(hi)
    zero = jnp.zeros((bsz, heads, BF16_ROWS - 6, seq), BF16)
    k_terms = jnp.concatenate([jnp.stack([hi, mid, lo, one, one, one], axis=2), zero], axis=2)
    q_terms = jnp.concatenate([jnp.stack([-one, -one, -one, hi, mid, lo], axis=2), zero], axis=2)
    used = ATT_HEADS * BF16_ROWS
    k_bias = jnp.transpose(k_terms.reshape(bsz, n_blk, used, seq), (0, 3, 1, 2))
    k_bias = jnp.concatenate([k_bias, jnp.zeros((bsz, seq, n_blk, LANES - used), BF16)], axis=3)
    q_bias = q_terms.reshape(bsz, n_blk, used, seq)
    return k_bias.reshape(bsz * seq, n_blk * LANES), q_bias


def _s5_kernel(u_ref, toep_h, toep_l, sb_h, sb_l, at_ref, ca_h, ca_l, y_ref, acc_ref, st_ref,
               *, t, row_chunk):
    n_chunks = st_ref.shape[0]
    half = st_ref.shape[1] // 2

    def split(x):
        hi = x.astype(BF16)
        return hi, (x - hi.astype(F32)).astype(BF16)

    def chunk_rows(ref, r, step):
        return ref.at[pl.ds(r * row_chunk * t + step, row_chunk, stride=t), :]

    def stage1(r, c):
        ut = jnp.concatenate([chunk_rows(u_ref, r, step)[...] for step in range(t)], axis=1)
        uh, ul = split(ut)
        sl = pl.ds(pl.multiple_of(r * row_chunk, 8), row_chunk)
        acc_ref[sl, :] = _dot3(uh, ul, toep_h[...], toep_l[...])
        st_ref[sl, :] = _dot3(uh, ul, sb_h[...], sb_l[...])
        return c

    lax.fori_loop(0, n_chunks // row_chunk, stage1, 0)

    a_re = at_ref[0:1, :]
    a_im = at_ref[1:2, :]

    def scan(c, state):
        x_re, x_im = state
        row = pl.ds(c, 1)
        s_re = st_ref[row, :half]
        s_im = st_ref[row, half:]
        st_ref[row, :half] = x_re
        st_ref[row, half:] = x_im
        return (a_re * x_re - a_im * x_im + s_re, a_re * x_im + a_im * x_re + s_im)

    zero = jnp.zeros((1, half), F32)
    lax.fori_loop(0, n_chunks, scan, (zero, zero), unroll=4)

    def stage3(r, c):
        sl = pl.ds(pl.multiple_of(r * row_chunk, 8), row_chunk)
        xh, xl = split(st_ref[sl, :])
        y = acc_ref[sl, :] + _dot3(xh, xl, ca_h[...], ca_l[...])
        y = 0.5 * y * (1.0 + jnp.tanh(math.sqrt(2.0 / math.pi) * (y + 0.044715 * (y * y * y))))
        for step in range(t):
            chunk_rows(y_ref, r, step)[...] = y[:, step * LANES:(step + 1) * LANES]
        return c

    lax.fori_loop(0, n_chunks // row_chunk, stage3, 0)


def _s5_operators(lam_re, lam_im, log_dt, b_re, b_im, c_re, c_im, d_skip):
    t = S5_T
    lre, lim = lam_re.astype(F32), lam_im.astype(F32)
    dt = jnp.exp(log_dt.astype(F32))[:, None]
    mag = jnp.exp(lre * dt)
    a_re, a_im = mag * jnp.cos(lim * dt), mag * jnp.sin(lim * dt)
    den = lre * lre + lim * lim
    g_re = ((a_re - 1.0) * lre + a_im * lim) / den
    g_im = (a_im * lre - (a_re - 1.0) * lim) / den
    br, bi = b_re.astype(F32), b_im.astype(F32)
    bb_re = g_re[..., None] * br - g_im[..., None] * bi
    bb_im = g_re[..., None] * bi + g_im[..., None] * br
    k = jnp.arange(t + 1, dtype=F32)[:, None, None]
    pw_mag = jnp.exp(k * (lre * dt)[None])
    pw_re = pw_mag * jnp.cos(k * (lim * dt)[None])
    pw_im = pw_mag * jnp.sin(k * (lim * dt)[None])
    cr, ci = c_re.astype(F32), c_im.astype(F32)
    hp = lax.Precision.HIGHEST
    ab_re = pw_re[..., None] * bb_re[None] - pw_im[..., None] * bb_im[None]
    ab_im = pw_re[..., None] * bb_im[None] + pw_im[..., None] * bb_re[None]
    kern = (jnp.einsum('ghp,kgpe->kghe', cr, ab_re, precision=hp)
            - jnp.einsum('ghp,kgpe->kghe', ci, ab_im, precision=hp))
    lag = jnp.arange(t)[None, :] - jnp.arange(t)[:, None]
    toep = jnp.where((lag >= 0)[:, :, None, None, None],
                     kern[jnp.clip(lag, 0, t)], 0.0)
    eye_t = jnp.eye(t, dtype=F32)
    eye_h = jnp.eye(S5_GROUP, dtype=F32)
    toep = toep + (eye_t[:, :, None, None, None] * eye_h[None, None, None]
                   * d_skip.astype(F32)[None, None, :, :, None])
    g_n = toep.shape[2]
    nb, gb = g_n // S5_BLOCK_GROUPS, S5_BLOCK_GROUPS
    eye_g = jnp.eye(gb, dtype=F32)
    width = t * gb * S5_GROUP
    toep = jnp.transpose(toep, (2, 0, 4, 1, 3)).reshape(nb, gb, t, S5_GROUP, t, S5_GROUP)
    toep = jnp.einsum('bgjaih,gk->bjgaikh', toep, eye_g).reshape(nb, width, width)
    rev = t - 1 - jnp.arange(t)

    def state_in(ab):
        x = jnp.transpose(ab[rev], (1, 0, 3, 2)).reshape(nb, gb, t, S5_GROUP, S5_STATE)
        return jnp.einsum('bgjap,gk->bjgakp', x, eye_g).reshape(nb, width, gb * S5_STATE)

    sb = jnp.concatenate([state_in(ab_re), state_in(ab_im)], axis=2)
    e_re, e_im = pw_re[1:], pw_im[1:]
    ca_re = cr[None] * e_re[:, :, None, :] - ci[None] * e_im[:, :, None, :]
    ca_im = -(cr[None] * e_im[:, :, None, :] + ci[None] * e_re[:, :, None, :])

    def state_out(ca):
        x = jnp.transpose(ca, (1, 3, 0, 2)).reshape(nb, gb, S5_STATE, t, S5_GROUP)
        return jnp.einsum('bgpih,gk->bgpikh', x, eye_g).reshape(nb, gb * S5_STATE, width)

    ca = jnp.concatenate([state_out(ca_re), state_out(ca_im)], axis=1)
    at = jnp.stack([pw_re[t].reshape(nb, gb * S5_STATE), pw_im[t].reshape(nb, gb * S5_STATE)], axis=1)
    return [*_split_bf16(toep), *_split_bf16(sb), at, *_split_bf16(ca)]


def _s5_layer(u, ops):
    bsz, seq, ch = u.shape
    t = S5_T
    n_chunks = seq // t
    width = t * LANES
    n_state = 2 * S5_BLOCK_GROUPS * S5_STATE
    row_chunk = next(n_chunks // n for n in (8, 4, 2, 1) if n_chunks % (8 * n) == 0)
    op_spec = lambda r, c: pl.BlockSpec((None, r, c), lambda p, b: (p, 0, 0),
                                        pipeline_mode=pl.Buffered(1))
    slab = pl.BlockSpec((None, seq, LANES), lambda p, b: (b, 0, p))
    return pl.pallas_call(
        functools.partial(_s5_kernel, t=t, row_chunk=row_chunk),
        out_shape=jax.ShapeDtypeStruct((bsz, seq, ch), F32),
        grid=(ch // LANES, bsz),
        in_specs=[slab, op_spec(width, width), op_spec(width, width),
                  op_spec(width, n_state), op_spec(width, n_state), op_spec(2, n_state // 2),
                  op_spec(n_state, width), op_spec(n_state, width)],
        out_specs=slab,
        scratch_shapes=[pltpu.VMEM((n_chunks, width), F32), pltpu.VMEM((n_chunks, n_state), F32)],
        compiler_params=_cparams(2),
        name="s5_chunked",
    )(u, *ops)


def _post_kernel(*refs, n_pad, ff_chunk, glu, final):
    h_ref, a_ref, b_ref = refs[:3]
    pos = 3
    if glu:
        wglu_ref, bglu_ref = refs[pos:pos + 2]
        pos += 2
    wo_ref, gf_ref, wgu_ref, wd_ref = refs[pos:pos + 4]
    pos += 4
    if final:
        gl_ref = refs[pos]
        pos += 1
    o_ref, acc_ref = refs[pos:pos + 2]

    tm = h_ref.shape[0]
    half = a_ref.shape[-1]
    d_ff = wd_ref.shape[0]
    row = lax.broadcasted_iota(jnp.int32, (tm, 1), 0) + pl.program_id(1) * tm
    valid = row >= n_pad

    a = a_ref[...]
    if glu:
        y = b_ref[...]
        z = _dot(y.astype(BF16), wglu_ref[...]) + bglu_ref[...]
        b = (y * (1.0 / (1.0 + jnp.exp(-z)))).astype(BF16)
    else:
        b = b_ref[...]
    mix = _dot(a, wo_ref[:half, :]) + _dot(b, wo_ref[half:, :])
    h1 = jnp.where(valid, h_ref[...] + mix, 0.0)
    hn = _rms(h1, gf_ref[...]).astype(BF16)
    acc_ref[...] = jnp.zeros_like(acc_ref)
    for c in range(0, d_ff, ff_chunk):
        gate = _dot(hn, wgu_ref[:, c:c + ff_chunk])
        up = _dot(hn, wgu_ref[:, d_ff + c:d_ff + c + ff_chunk])
        act = (gate * (1.0 / (1.0 + jnp.exp(-gate))) * up).astype(BF16)
        acc_ref[...] += _dot(act, wd_ref[c:c + ff_chunk, :])
    h2 = jnp.where(valid, h1 + acc_ref[...], 0.0)
    if final:
        h2 = _rms(h2, gl_ref[...])
    o_ref[...] = h2


def _post_mixer(h, a, b, w_out, g_ffn, w_gu, w_down, glu=None, final_g=None):
    bsz, seq, d = h.shape
    tm = next(t for t in (768, 512, 256) if seq % t == 0)
    half = a.shape[-1]
    d_ff = w_down.shape[0]
    row_spec = lambda w: pl.BlockSpec((None, tm, w), lambda bi, i: (bi, i, 0))
    const = lambda shape: _resident(shape, lambda bi, i: (0,) * len(shape))
    args = [h, a, b]
    in_specs = [row_spec(d), row_spec(half), row_spec(half)]
    if glu is not None:
        args += list(glu)
        in_specs += [const((half, half)), const((1, half))]
    args += [w_out, g_ffn, w_gu, w_down]
    in_specs += [const((2 * half, d)), const((1, d)), const((d, 2 * d_ff)), const((d_ff, d))]
    if final_g is not None:
        args.append(final_g)
        in_specs.append(const((1, d)))
    return pl.pallas_call(
        functools.partial(_post_kernel, n_pad=N_PAD, ff_chunk=256,
                          glu=glu is not None, final=final_g is not None),
        out_shape=jax.ShapeDtypeStruct((bsz, seq, d), F32),
        grid=(bsz, seq // tm),
        in_specs=in_specs,
        out_specs=row_spec(d),
        scratch_shapes=[pltpu.VMEM((tm, d), F32)],
        compiler_params=_cparams(2),
        name="outproj_ffn",
    )(*args)


def kernel(x, meta_tokens, norm_mix_g, norm_ffn_g, final_norm_g, even_w_in, even_w_out,
           diff_lam_q1, diff_lam_k1, diff_lam_q2, diff_lam_k2, diff_subln_g, odd_w_in,
           odd_w_out, fox_b_f, s5_lam_re, s5_lam_im, s5_log_dt, s5_b_re, s5_b_im, s5_c_re,
           s5_c_im, s5_d, s5_w_glu, s5_b_glu, ffn_w_gate_up, ffn_w_down):
    bsz, n_seq, d = x.shape
    seq = n_seq + FRONT
    half = d // 2
    n_blk = half // ATT_BLOCK
    scale = HEAD_DIM ** -0.5
    fox_heads = half // HEAD_DIM

    pad = jnp.zeros((bsz, N_PAD, d), x.dtype)
    meta = jnp.broadcast_to(meta_tokens[None].astype(x.dtype), (bsz, N_META, d))
    h = jnp.concatenate([pad, meta, x], axis=1)

    w_in = even_w_in[0]
    col = lambda n: w_in[:, n * half:(n + 1) * half]
    w_tok = jnp.concatenate([col(1), col(4)], axis=1).astype(BF16)
    w_tr = jnp.concatenate([col(0) * scale, col(2), col(3) * scale, col(5)], axis=1).T.astype(BF16)
    tok, tr = _norm_proj(h.reshape(bsz * seq, d), norm_mix_g[0][None, :], w_tok, w_tr)
    sb = _attention("sb", tok, tr, bsz, seq, 0, 0, n_blk, n_blk)
    lam_init = 0.8 - 0.6 * math.exp(-0.3 * 0)
    lam = (jnp.exp(jnp.sum(diff_lam_q1[0].astype(F32) * diff_lam_k1[0].astype(F32)))
           - jnp.exp(jnp.sum(diff_lam_q2[0].astype(F32) * diff_lam_k2[0].astype(F32))) + lam_init)
    g_col = jnp.broadcast_to(diff_subln_g[0].astype(F32)[:, None], (2 * HEAD_DIM, ATT_TILE))
    df = _attention("diff", tok, tr, bsz, seq, n_blk, 2 * n_blk, 3 * n_blk, n_blk,
                    (lam.reshape(1, 1).astype(F32), g_col), out_scale=1.0 - lam_init)
    h = _post_mixer(h, sb.reshape(bsz, seq, half), df.reshape(bsz, seq, half),
                    even_w_out[0].astype(BF16), norm_ffn_g[0][None, :],
                    ffn_w_gate_up[0].astype(BF16), ffn_w_down[0].astype(BF16))

    w_in = odd_w_in[0]
    col = lambda n: w_in[:, n * half:(n + 1) * half]
    w_tok = col(1).astype(BF16)
    w_tr = jnp.concatenate([col(0) * scale, col(2)], axis=1).T.astype(BF16)
    w_f = w_in[:, 3 * half:3 * half + fox_heads]
    w_u = w_in[:, 3 * half + fox_heads:].astype(BF16)
    f_rows = BF16_ROWS
    w_ft = jnp.zeros((f_rows, d), F32).at[:fox_heads].set(w_f.T)
    tok, tr, u, ft = _norm_proj(h.reshape(bsz * seq, d), norm_mix_g[1][None, :], w_tok, w_tr,
                                odd_weights=(w_u,) + _split_bf16(w_ft))
    ft = jnp.transpose(ft.reshape(f_rows, bsz, seq), (1, 0, 2))
    b_f = jnp.zeros((f_rows, 1), F32).at[:fox_heads, 0].set(fox_b_f[0].astype(F32))
    f_row = _forget_cumsum(ft, b_f)[:, :fox_heads]
    fox = _attention("fox", tok, tr, bsz, seq, 0, 0, n_blk, n_blk,
                     _forget_bias_operands(f_row, n_blk))
    ops = _s5_operators(s5_lam_re[0], s5_lam_im[0], s5_log_dt[0], s5_b_re[0], s5_b_im[0],
                        s5_c_re[0], s5_c_im[0], s5_d[0])
    ssm = _s5_layer(u.reshape(bsz, seq, -1), ops)
    h = _post_mixer(h, fox.reshape(bsz, seq, half), ssm, odd_w_out[0].astype(BF16),
                    norm_ffn_g[1][None, :], ffn_w_gate_up[1].astype(BF16),
                    ffn_w_down[1].astype(BF16),
                    glu=(s5_w_glu[0].astype(BF16), s5_b_glu[0][None, :].astype(F32)),
                    final_g=final_norm_g[None, :])
    return h[:, FRONT:]
```

```python
import functools
import math

import jax
import jax.numpy as jnp
from jax import lax
from jax.experimental import pallas as pl
from jax.experimental.pallas import tpu as pltpu

F32 = jnp.float32
BF16 = jnp.bfloat16

HEAD_DIM = 64
LANES = 128
ATT_BLOCK = 256
ATT_HEADS = ATT_BLOCK // HEAD_DIM
BF16_ROWS = 16
N_META = 16
ATT_TILE = 256
ATT_KTILE = 768
SB_BLOCK = 128
FRONT = ATT_TILE
N_PAD = FRONT - N_META
NEG = -1e30
RMS_EPS = 1e-6
SB_DEAD_LOG = -760.0
S5_GROUP = 16
S5_STATE = 64
S5_T = 8
S5_BLOCK_GROUPS = LANES // S5_GROUP
VMEM_LIMIT = 56 * 1024 * 1024


def _cparams(n_axes):
    return pltpu.CompilerParams(
        dimension_semantics=("arbitrary",) * n_axes, vmem_limit_bytes=VMEM_LIMIT)


def _resident(shape, index_map):
    return pl.BlockSpec(shape, index_map, pipeline_mode=pl.Buffered(1))


def _trunc_bf16(x):
    bits = lax.bitcast_convert_type(x, jnp.uint32) & jnp.uint32(0xFFFF0000)
    return lax.bitcast_convert_type(bits, F32)


def _split_bf16(w):
    hi = _trunc_bf16(w)
    return hi.astype(BF16), (w - hi).astype(BF16)


def _dot(a, b):
    return jnp.dot(a, b, preferred_element_type=F32)


def _dot_nt(a, b):
    return lax.dot_general(a, b, (((1,), (1,)), ((), ())), preferred_element_type=F32)


def _dot3(a_hi, a_lo, b_hi, b_lo):
    return _dot(a_hi, b_hi) + _dot(a_lo, b_hi) + _dot(a_hi, b_lo)


def _rms(x, g):
    ms = jnp.mean(x * x, axis=-1, keepdims=True)
    return x * lax.rsqrt(ms + RMS_EPS) * g


def _proj_kernel(*refs, n_chunk, odd):
    if odd:
        (x_ref, g_ref, wtok_ref, wtr_ref, wu_ref, wfh_ref, wfl_ref,
         tok_ref, tr_ref, u_ref, ft_ref) = refs
    else:
        x_ref, g_ref, wtok_ref, wtr_ref, tok_ref, tr_ref = refs
    hn32 = _rms(x_ref[...], g_ref[...])
    hn = hn32.astype(BF16)
    for c in range(0, tok_ref.shape[-1], n_chunk):
        tok_ref[:, c:c + n_chunk] = _dot(hn, wtok_ref[:, c:c + n_chunk]).astype(tok_ref.dtype)
    for c in range(0, tr_ref.shape[0], n_chunk):
        tr_ref[c:c + n_chunk, :] = _dot_nt(wtr_ref[c:c + n_chunk, :], hn).astype(tr_ref.dtype)
    if odd:
        u_ref[...] = _dot(hn, wu_ref[...])
        hn_lo = (hn32 - hn.astype(F32)).astype(BF16)
        wfh = wfh_ref[...]
        ft_ref[...] = _dot_nt(wfh, hn) + _dot_nt(wfh, hn_lo) + _dot_nt(wfl_ref[...], hn)


def _norm_proj(h, g, w_tok, w_tr, odd_weights=None):
    m, d = h.shape
    tm = next(t for t in (512, 256) if m % t == 0)
    n_tok, n_tr = w_tok.shape[1], w_tr.shape[0]
    const = lambda shape: _resident(shape, lambda i: (0, 0))
    args = [h, g, w_tok, w_tr]
    in_specs = [pl.BlockSpec((tm, d), lambda i: (i, 0)), const((1, d)),
                const((d, n_tok)), const((n_tr, d))]
    out_shape = [jax.ShapeDtypeStruct((m, n_tok), BF16), jax.ShapeDtypeStruct((n_tr, m), BF16)]
    out_specs = [pl.BlockSpec((tm, n_tok), lambda i: (i, 0)),
                 pl.BlockSpec((n_tr, tm), lambda i: (0, i))]
    if odd_weights is not None:
        wu, wfh, wfl = odd_weights
        nu, nf = wu.shape[1], wfh.shape[0]
        args += [wu, wfh, wfl]
        in_specs += [const((d, nu)), const((nf, d)), const((nf, d))]
        out_shape += [jax.ShapeDtypeStruct((m, nu), F32), jax.ShapeDtypeStruct((nf, m), F32)]
        out_specs += [pl.BlockSpec((tm, nu), lambda i: (i, 0)),
                      pl.BlockSpec((nf, tm), lambda i: (0, i))]
    return pl.pallas_call(
        functools.partial(_proj_kernel, n_chunk=256, odd=odd_weights is not None),
        out_shape=tuple(out_shape),
        grid=(m // tm,),
        in_specs=in_specs,
        out_specs=tuple(out_specs),
        compiler_params=_cparams(1),
        name="norm_proj_odd" if odd_weights is not None else "norm_proj_even",
    )(*args)


def _head_rows(h):
    return slice(h * HEAD_DIM, (h + 1) * HEAD_DIM)


def _head_queries(qt):
    zero = jnp.zeros((HEAD_DIM, qt.shape[1]), qt.dtype)
    return [jnp.concatenate([qt[_head_rows(r)] if r == h else zero for r in range(ATT_HEADS)],
                            axis=0) for h in range(ATT_HEADS)]


def _tile_mask(kind, start, off, tq, tk, n_pad, strict):
    if kind == "mid":
        return None
    row = lax.broadcasted_iota(jnp.int32, (tk, 1), 0)
    if kind == "first":
        return row >= n_pad
    dist = (lax.broadcasted_iota(jnp.int32, (tk, tq), 0)
            - lax.broadcasted_iota(jnp.int32, (tk, tq), 1))
    causal = (dist < off) if strict else (dist <= off)
    return causal & (row + start >= n_pad)


def _pipelined_tiles(n, produce, consume):
    produce(0, 0)

    @pl.when(n > 0)
    def _():
        produce(1, 1)
        consume(0, 0, "first")

    def body(k, carry):
        for parity in range(2):
            @pl.when(k % 2 == parity)
            def _():
                produce(1 - parity, k + 1)
                consume(parity, k, "mid")
        return carry

    lax.fori_loop(1, n, body, 0)
    for parity in range(2):
        @pl.when(n % 2 == parity)
        def _():
            consume(parity, n, "diag")


def _sb_kernel(k_ref, qt_ref, vt_ref, tri_ref, o_ref, acc_ref, carry_ref, *, tq, tk, blk, n_pad):
    i = pl.program_id(2)
    ratio = tk // tq
    c_diag = i // ratio
    off = (i - c_diag * ratio) * tq
    n_blocks = tk // blk
    q_cat = jnp.concatenate(_head_queries(qt_ref[...]), axis=1)
    tri = tri_ref[...]
    acc_ref[...] = jnp.zeros_like(acc_ref)
    carry_ref[...] = jnp.zeros_like(carry_ref)

    def step(c, kind):
        start = pl.multiple_of(c * tk, tk)
        ks = k_ref[pl.ds(start, tk), :]
        vt = vt_ref[:, pl.ds(start, tk)]
        mask = _tile_mask(kind, start, off, tq, tk, n_pad, strict=True)
        logits = _dot(ks, q_cat)
        for h in range(ATT_HEADS):
            x = logits[:, h * tq:(h + 1) * tq]
            sp = jnp.maximum(x, 0.0) + jnp.log(1.0 + jnp.exp(-jnp.abs(x)))
            log_beta = x - sp
            if mask is not None:
                sp = jnp.where(mask, sp, 0.0)
            hi = sp.astype(BF16)
            lo = (sp - hi.astype(F32)).astype(BF16)
            blocks = [slice(b * blk, (b + 1) * blk) for b in range(n_blocks)]
            within = _dot(tri, jnp.concatenate(
                [jnp.concatenate([hi[r], lo[r]], axis=0) for r in blocks], axis=1))
            run = carry_ref[h]
            w_blocks = [None] * n_blocks
            for b in reversed(range(n_blocks)):
                r = blocks[b]
                w_blocks[b] = jnp.exp(log_beta[r] + within[:, b * tq:(b + 1) * tq] + run)
                run = run - jnp.sum(sp[r], axis=0, keepdims=True)
            carry_ref[h] = run
            w = jnp.concatenate(w_blocks, axis=0)
            if mask is not None:
                w = jnp.where(mask, w, 0.0)
            rows = _head_rows(h)
            acc_ref[rows, :] += _dot(vt[rows, :], w.astype(BF16))

    def live():
        return (jnp.max(carry_ref[...]) > SB_DEAD_LOG).astype(jnp.int32)

    step(c_diag, "diag")

    def mid(state):
        n, _ = state
        step(c_diag - n, "mid")
        return n + 1, live()

    _, alive = lax.while_loop(lambda s: (s[0] < c_diag) & (s[1] > 0), mid, (jnp.int32(1), live()))

    @pl.when((c_diag > 0) & (alive > 0))
    def _():
        step(0, "first")

    o_ref[...] = acc_ref[...].T.astype(o_ref.dtype)


def _fox_queries(qt, qb):
    tq = qt.shape[1]
    zeros = lambda n: jnp.zeros((n, tq), qt.dtype)
    out = []
    for h in range(ATT_HEADS):
        own = [qt[_head_rows(h)], zeros(HEAD_DIM)]
        pieces = (own if h % 2 == 0 else own[::-1]) + [
            zeros(h * BF16_ROWS), qb[h * BF16_ROWS:(h + 1) * BF16_ROWS],
            zeros(2 * HEAD_DIM - (h + 1) * BF16_ROWS)]
        out.append(jnp.concatenate([p for p in pieces if p.shape[0]], axis=0))
    return out


def _softmax_attn_kernel(*refs, tq, tk, n_pad, mode, out_scale):
    if mode == "fox":
        k_ref, kb_ref, qt_ref, qb_ref, vt_ref, o_ref, acc_ref, m_ref, sc_ref = refs
        q_heads = _fox_queries(qt_ref[...], qb_ref[...])
        q_cats = [jnp.concatenate(q_heads[2 * p:2 * p + 2], axis=1) for p in range(ATT_HEADS // 2)]
    else:
        k_ref, qt_ref, vt_ref, lam_ref, g_ref, o_ref, acc_ref, m_ref, sc_ref = refs
        q_cats = [jnp.concatenate(_head_queries(qt_ref[...]), axis=1)]
    i = pl.program_id(2)
    ratio = tk // tq
    c_diag = i // ratio
    off = (i - c_diag * ratio) * tq
    ones_rows = (lax.broadcasted_iota(jnp.int32, (BF16_ROWS, tk), 0) == 0).astype(BF16)
    v_rows = HEAD_DIM if mode == "fox" else 2 * HEAD_DIM
    acc_ref[...] = jnp.zeros_like(acc_ref)
    m_ref[...] = jnp.full_like(m_ref, NEG)

    def produce(slot, c):
        rows = pl.ds(pl.multiple_of(c * tk, tk), tk)
        ks = k_ref[rows, :]
        if mode == "fox":
            kb = kb_ref[rows, :]
            k_ops = [jnp.concatenate([ks[:, p * LANES:(p + 1) * LANES], kb], axis=1)
                     for p in range(ATT_HEADS // 2)]
        else:
            k_ops = [ks]
        width = ATT_HEADS * tq // len(q_cats)
        for n, (k_op, q) in enumerate(zip(k_ops, q_cats)):
            sc_ref[slot, :, n * width:(n + 1) * width] = _dot(k_op, q)

    def consume(slot, c, kind):
        start = pl.multiple_of(c * tk, tk)
        vt = vt_ref[:, pl.ds(start, tk)]
        mask = _tile_mask(kind, start, off, tq, tk, n_pad, strict=False)
        for h in range(ATT_HEADS):
            s = sc_ref[slot, :, h * tq:(h + 1) * tq]
            if mask is not None:
                s = jnp.where(mask, s, NEG)
            m_old = m_ref[h]
            m_new = jnp.maximum(m_old, jnp.max(s, axis=0, keepdims=True))
            alpha = jnp.exp(m_old - m_new)
            p = jnp.exp(s - m_new).astype(BF16)
            m_ref[h] = m_new
            vh = vt[_head_rows(h)] if mode == "fox" else vt[(h // 2) * v_rows:(h // 2 + 1) * v_rows]
            v_aug = jnp.concatenate([vh, ones_rows], axis=0)
            acc_ref[h] = alpha * acc_ref[h] + _dot(v_aug, p)

    _pipelined_tiles(c_diag, produce, consume)

    outs = []
    for h in range(ATT_HEADS):
        acc = acc_ref[h]
        outs.append(acc[:v_rows] * (1.0 / acc[v_rows:v_rows + 1]))
    if mode == "diff":
        heads = []
        for d in range(ATT_HEADS // 2):
            out_d = outs[2 * d] - lam_ref[0, 0] * outs[2 * d + 1]
            ms = jnp.mean(out_d * out_d, axis=0, keepdims=True)
            heads.append(out_d * lax.rsqrt(ms + RMS_EPS) * g_ref[...] * out_scale)
        outs = heads
    o_ref[...] = jnp.concatenate(outs, axis=0).T.astype(o_ref.dtype)


def _attention(mode, tok, tr, bsz, seq, k_blk, q_blk, v_blk, n_blk, extra=(), out_scale=1.0):
    tq, tk = ATT_TILE, ATT_KTILE
    nq = seq // tq
    k_spec = pl.BlockSpec((seq, ATT_BLOCK), lambda b, p, i: (b, k_blk + p))
    qt_spec = pl.BlockSpec((ATT_BLOCK, tq), lambda b, p, i: (q_blk + p, b * nq + i))
    vt_spec = pl.BlockSpec((ATT_BLOCK, seq), lambda b, p, i: (v_blk + p, b))
    m_scratch = pltpu.VMEM((ATT_HEADS, 1, tq), F32)
    sc_scratch = pltpu.VMEM((2, tk, ATT_HEADS * tq), F32)
    if mode == "sb":
        idx = jnp.arange(SB_BLOCK)
        tri = -(idx[None, :] > idx[:, None]).astype(BF16)
        args = [tok, tr, tr, jnp.concatenate([tri, tri], axis=1)]
        in_specs = [k_spec, qt_spec, vt_spec,
                    _resident((SB_BLOCK, 2 * SB_BLOCK), lambda b, p, i: (0, 0))]
        scratch = [pltpu.VMEM((ATT_BLOCK, tq), F32), m_scratch]
        body = functools.partial(_sb_kernel, tq=tq, tk=tk, blk=SB_BLOCK, n_pad=N_PAD)
    elif mode == "fox":
        k_bias, q_bias = extra
        args = [tok, k_bias, tr, q_bias, tr]
        in_specs = [k_spec, pl.BlockSpec((seq, LANES), lambda b, p, i: (b, p)), qt_spec,
                    pl.BlockSpec((None, None, ATT_HEADS * BF16_ROWS, tq), lambda b, p, i: (b, p, 0, i)),
                    vt_spec]
        scratch = [pltpu.VMEM((ATT_HEADS, HEAD_DIM + BF16_ROWS, tq), F32), m_scratch, sc_scratch]
        body = functools.partial(_softmax_attn_kernel, tq=tq, tk=tk, n_pad=N_PAD, mode=mode,
                                 out_scale=out_scale)
    else:
        lam, g_col = extra
        args = [tok, tr, tr, lam, g_col]
        in_specs = [k_spec, qt_spec, vt_spec, pl.BlockSpec(memory_space=pltpu.SMEM),
                    _resident((2 * HEAD_DIM, tq), lambda b, p, i: (0, 0))]
        scratch = [pltpu.VMEM((ATT_HEADS, 2 * HEAD_DIM + BF16_ROWS, tq), F32), m_scratch, sc_scratch]
        body = functools.partial(_softmax_attn_kernel, tq=tq, tk=tk, n_pad=N_PAD, mode=mode,
                                 out_scale=out_scale)
    return pl.pallas_call(
        body,
        out_shape=jax.ShapeDtypeStruct((bsz * seq, n_blk * ATT_BLOCK), BF16),
        grid=(bsz, n_blk, nq),
        in_specs=in_specs,
        out_specs=pl.BlockSpec((tq, ATT_BLOCK), lambda b, p, i: (b * nq + i, p)),
        scratch_shapes=scratch,
        compiler_params=_cparams(3),
        name=mode + "_attention",
    )(*args)


def _fcum_kernel(ft_ref, b_ref, tri_ref, o_ref, *, tile, n_pad):
    seq = ft_ref.shape[-1]
    tri = tri_ref[...]
    carry = jnp.zeros((ft_ref.shape[0], 1), F32)
    for c in range(seq // tile):
        x = ft_ref[:, c * tile:(c + 1) * tile] + b_ref[...]
        log_f = jnp.minimum(x, 0.0) - jnp.log(1.0 + jnp.exp(-jnp.abs(x)))
        pos = lax.broadcasted_iota(jnp.int32, (1, tile), 1) + c * tile
        log_f = jnp.where(pos >= n_pad, log_f, 0.0)
        hi = log_f.astype(BF16)
        r1 = log_f - hi.astype(F32)
        mid = r1.astype(BF16)
        lo = (r1 - mid.astype(F32)).astype(BF16)
        cs = _dot(jnp.concatenate([hi, mid, lo], axis=1), tri)
        o_ref[:, c * tile:(c + 1) * tile] = cs + carry
        carry = carry + jnp.sum(log_f, axis=1, keepdims=True)


def _forget_cumsum(ft, b_f):
    bsz, rows, seq = ft.shape
    tile = ATT_TILE
    idx = jnp.arange(tile)
    tri = (idx[:, None] <= idx[None, :]).astype(BF16)
    tri3 = jnp.concatenate([tri, tri, tri], axis=0)
    return pl.pallas_call(
        functools.partial(_fcum_kernel, tile=tile, n_pad=N_PAD),
        out_shape=jax.ShapeDtypeStruct((bsz, rows, seq), F32),
        grid=(bsz,),
        in_specs=[pl.BlockSpec((None, rows, seq), lambda b: (b, 0, 0)),
                  _resident((rows, 1), lambda b: (0, 0)),
                  _resident((3 * tile, tile), lambda b: (0, 0))],
        out_specs=pl.BlockSpec((None, rows, seq), lambda b: (b, 0, 0)),
        compiler_params=_cparams(1),
        name="forget_cumsum",
    )(ft, b_f, tri3)


def _forget_bias_operands(f_row, n_blk):
    bsz, heads, seq = f_row.shape
    hi = _trunc_bf16(f_row)
    mid = _trunc_bf16(f_row - hi)
    lo = ((f_row - hi) - mid).astype(BF16)
    hi, mid = hi.astype(BF16), mid.astype(BF16)
    one = jnp.ones_like(hi)
    zero = jnp.zeros((bsz, heads, BF16_ROWS - 6, seq), BF16)
    k_terms = jnp.concatenate([jnp.stack([hi, mid, lo, one, one, one], axis=2), zero], axis=2)
    q_terms = jnp.concatenate([jnp.stack([-one, -one, -one, hi, mid, lo], axis=2), zero], axis=2)
    used = ATT_HEADS * BF16_ROWS
    k_bias = jnp.transpose(k_terms.reshape(bsz, n_blk, used, seq), (0, 3, 1, 2))
    k_bias = jnp.concatenate([k_bias, jnp.zeros((bsz, seq, n_blk, LANES - used), BF16)], axis=3)
    q_bias = q_terms.reshape(bsz, n_blk, used, seq)
    return k_bias.reshape(bsz * seq, n_blk * LANES), q_bias


def _s5_kernel(u_ref, toep_h, toep_l, sb_h, sb_l, at_ref, ca_h, ca_l, y_ref, acc_ref, st_ref,
               *, t, row_chunk):
    n_chunks = st_ref.shape[0]
    half = st_ref.shape[1] // 2

    def split(x):
        hi = x.astype(BF16)
        return hi, (x - hi.astype(F32)).astype(BF16)

    def chunk_rows(ref, r, step):
        return ref.at[pl.ds(r * row_chunk * t + step, row_chunk, stride=t), :]

    def stage1(r, c):
        ut = jnp.concatenate([chunk_rows(u_ref, r, step)[...] for step in range(t)], axis=1)
        uh, ul = split(ut)
        sl = pl.ds(pl.multiple_of(r * row_chunk, 8), row_chunk)
        acc_ref[sl, :] = _dot3(uh, ul, toep_h[...], toep_l[...])
        st_ref[sl, :] = _dot3(uh, ul, sb_h[...], sb_l[...])
        return c

    lax.fori_loop(0, n_chunks // row_chunk, stage1, 0)

    a_re = at_ref[0:1, :]
    a_im = at_ref[1:2, :]

    def scan(c, state):
        x_re, x_im = state
        row = pl.ds(c, 1)
        s_re = st_ref[row, :half]
        s_im = st_ref[row, half:]
        st_ref[row, :half] = x_re
        st_ref[row, half:] = x_im
        return (a_re * x_re - a_im * x_im + s_re, a_re * x_im + a_im * x_re + s_im)

    zero = jnp.zeros((1, half), F32)
    lax.fori_loop(0, n_chunks, scan, (zero, zero), unroll=4)

    def stage3(r, c):
        sl = pl.ds(pl.multiple_of(r * row_chunk, 8), row_chunk)
        xh, xl = split(st_ref[sl, :])
        y = acc_ref[sl, :] + _dot3(xh, xl, ca_h[...], ca_l[...])
        y = 0.5 * y * (1.0 + jnp.tanh(math.sqrt(2.0 / math.pi) * (y + 0.044715 * (y * y * y))))
        for step in range(t):
            chunk_rows(y_ref, r, step)[...] = y[:, step * LANES:(step + 1) * LANES]
        return c

    lax.fori_loop(0, n_chunks // row_chunk, stage3, 0)


def _s5_operators(lam_re, lam_im, log_dt, b_re, b_im, c_re, c_im, d_skip):
    t = S5_T
    lre, lim = lam_re.astype(F32), lam_im.astype(F32)
    dt = jnp.exp(log_dt.astype(F32))[:, None]
    mag = jnp.exp(lre * dt)
    a_re, a_im = mag * jnp.cos(lim * dt), mag * jnp.sin(lim * dt)
    den = lre * lre + lim * lim
    g_re = ((a_re - 1.0) * lre + a_im * lim) / den
    g_im = (a_im * lre - (a_re - 1.0) * lim) / den
    br, bi = b_re.astype(F32), b_im.astype(F32)
    bb_re = g_re[..., None] * br - g_im[..., None] * bi
    bb_im = g_re[..., None] * bi + g_im[..., None] * br
    k = jnp.arange(t + 1, dtype=F32)[:, None, None]
    pw_mag = jnp.exp(k * (lre * dt)[None])
    pw_re = pw_mag * jnp.cos(k * (lim * dt)[None])
    pw_im = pw_mag * jnp.sin(k * (lim * dt)[None])
    cr, ci = c_re.astype(F32), c_im.astype(F32)
    hp = lax.Precision.HIGHEST
    ab_re = pw_re[..., None] * bb_re[None] - pw_im[..., None] * bb_im[None]
    ab_im = pw_re[..., None] * bb_im[None] + pw_im[..., None] * bb_re[None]
    kern = (jnp.einsum('ghp,kgpe->kghe', cr, ab_re, precision=hp)
            - jnp.einsum('ghp,kgpe->kghe', ci, ab_im, precision=hp))
    lag = jnp.arange(t)[None, :] - jnp.arange(t)[:, None]
    toep = jnp.where((lag >= 0)[:, :, None, None, None],
                     kern[jnp.clip(lag, 0, t)], 0.0)
    eye_t = jnp.eye(t, dtype=F32)
    eye_h = jnp.eye(S5_GROUP, dtype=F32)
    toep = toep + (eye_t[:, :, None, None, None] * eye_h[None, None, None]
                   * d_skip.astype(F32)[None, None, :, :, None])
    g_n = toep.shape[2]
    nb, gb = g_n // S5_BLOCK_GROUPS, S5_BLOCK_GROUPS
    eye_g = jnp.eye(gb, dtype=F32)
    width = t * gb * S5_GROUP
    toep = jnp.transpose(toep, (2, 0, 4, 1, 3)).reshape(nb, gb, t, S5_GROUP, t, S5_GROUP)
    toep = jnp.einsum('bgjaih,gk->bjgaikh', toep, eye_g).reshape(nb, width, width)
    rev = t - 1 - jnp.arange(t)

    def state_in(ab):
        x = jnp.transpose(ab[rev], (1, 0, 3, 2)).reshape(nb, gb, t, S5_GROUP, S5_STATE)
        return jnp.einsum('bgjap,gk->bjgakp', x, eye_g).reshape(nb, width, gb * S5_STATE)

    sb = jnp.concatenate([state_in(ab_re), state_in(ab_im)], axis=2)
    e_re, e_im = pw_re[1:], pw_im[1:]
    ca_re = cr[None] * e_re[:, :, None, :] - ci[None] * e_im[:, :, None, :]
    ca_im = -(cr[None] * e_im[:, :, None, :] + ci[None] * e_re[:, :, None, :])

    def state_out(ca):
        x = jnp.transpose(ca, (1, 3, 0, 2)).reshape(nb, gb, S5_STATE, t, S5_GROUP)
        return jnp.einsum('bgpih,gk->bgpikh', x, eye_g).reshape(nb, gb * S5_STATE, width)

    ca = jnp.concatenate([state_out(ca_re), state_out(ca_im)], axis=1)
    at = jnp.stack([pw_re[t].reshape(nb, gb * S5_STATE), pw_im[t].reshape(nb, gb * S5_STATE)], axis=1)
    return [*_split_bf16(toep), *_split_bf16(sb), at, *_split_bf16(ca)]


def _s5_layer(u, ops):
    bsz, seq, ch = u.shape
    t = S5_T
    n_chunks = seq // t
    width = t * LANES
    n_state = 2 * S5_BLOCK_GROUPS * S5_STATE
    row_chunk = next(n_chunks // n for n in (8, 4, 2, 1) if n_chunks % (8 * n) == 0)
    op_spec = lambda r, c: pl.BlockSpec((None, r, c), lambda p, b: (p, 0, 0),
                                        pipeline_mode=pl.Buffered(1))
    slab = pl.BlockSpec((None, seq, LANES), lambda p, b: (b, 0, p))
    return pl.pallas_call(
        functools.partial(_s5_kernel, t=t, row_chunk=row_chunk),
        out_shape=jax.ShapeDtypeStruct((bsz, seq, ch), F32),
        grid=(ch // LANES, bsz),
        in_specs=[slab, op_spec(width, width), op_spec(width, width),
                  op_spec(width, n_state), op_spec(width, n_state), op_spec(2, n_state // 2),
                  op_spec(n_state, width), op_spec(n_state, width)],
        out_specs=slab,
        scratch_shapes=[pltpu.VMEM((n_chunks, width), F32), pltpu.VMEM((n_chunks, n_state), F32)],
        compiler_params=_cparams(2),
        name="s5_chunked",
    )(u, *ops)


def _post_kernel(*refs, n_pad, ff_chunk, glu, final):
    h_ref, a_ref, b_ref = refs[:3]
    pos = 3
    if glu:
        wglu_ref, bglu_ref = refs[pos:pos + 2]
        pos += 2
    wo_ref, gf_ref, wgu_ref, wd_ref = refs[pos:pos + 4]
    pos += 4
    if final:
        gl_ref = refs[pos]
        pos += 1
    o_ref, acc_ref = refs[pos:pos + 2]

    tm = h_ref.shape[0]
    half = a_ref.shape[-1]
    d_ff = wd_ref.shape[0]
    row = lax.broadcasted_iota(jnp.int32, (tm, 1), 0) + pl.program_id(1) * tm
    valid = row >= n_pad

    a = a_ref[...]
    if glu:
        y = b_ref[...]
        z = _dot(y.astype(BF16), wglu_ref[...]) + bglu_ref[...]
        b = (y * (1.0 / (1.0 + jnp.exp(-z)))).astype(BF16)
    else:
        b = b_ref[...]
    mix = _dot(a, wo_ref[:half, :]) + _dot(b, wo_ref[half:, :])
    h1 = jnp.where(valid, h_ref[...] + mix, 0.0)
    hn = _rms(h1, gf_ref[...]).astype(BF16)
    acc_ref[...] = jnp.zeros_like(acc_ref)
    for c in range(0, d_ff, ff_chunk):
        gate = _dot(hn, wgu_ref[:, c:c + ff_chunk])
        up = _dot(hn, wgu_ref[:, d_ff + c:d_ff + c + ff_chunk])
        act = (gate * (1.0 / (1.0 + jnp.exp(-gate))) * up).astype(BF16)
        acc_ref[...] += _dot(act, wd_ref[c:c + ff_chunk, :])
    h2 = jnp.where(valid, h1 + acc_ref[...], 0.0)
    if final:
        h2 = _rms(h2, gl_ref[...])
    o_ref[...] = h2


def _post_mixer(h, a, b, w_out, g_ffn, w_gu, w_down, glu=None, final_g=None):
    bsz, seq, d = h.shape
    tm = next(t for t in (768, 512, 256) if seq % t == 0)
    half = a.shape[-1]
    d_ff = w_down.shape[0]
    row_spec = lambda w: pl.BlockSpec((None, tm, w), lambda bi, i: (bi, i, 0))
    const = lambda shape: _resident(shape, lambda bi, i: (0,) * len(shape))
    args = [h, a, b]
    in_specs = [row_spec(d), row_spec(half), row_spec(half)]
    if glu is not None:
        args += list(glu)
        in_specs += [const((half, half)), const((1, half))]
    args += [w_out, g_ffn, w_gu, w_down]
    in_specs += [const((2 * half, d)), const((1, d)), const((d, 2 * d_ff)), const((d_ff, d))]
    if final_g is not None:
        args.append(final_g)
        in_specs.append(const((1, d)))
    return pl.pallas_call(
        functools.partial(_post_kernel, n_pad=N_PAD, ff_chunk=256,
                          glu=glu is not None, final=final_g is not None),
        out_shape=jax.ShapeDtypeStruct((bsz, seq, d), F32),
        grid=(bsz, seq // tm),
        in_specs=in_specs,
        out_specs=row_spec(d),
        scratch_shapes=[pltpu.VMEM((tm, d), F32)],
        compiler_params=_cparams(2),
        name="outproj_ffn",
    )(*args)


def kernel(x, meta_tokens, norm_mix_g, norm_ffn_g, final_norm_g, even_w_in, even_w_out,
           diff_lam_q1, diff_lam_k1, diff_lam_q2, diff_lam_k2, diff_subln_g, odd_w_in,
           odd_w_out, fox_b_f, s5_lam_re, s5_lam_im, s5_log_dt, s5_b_re, s5_b_im, s5_c_re,
           s5_c_im, s5_d, s5_w_glu, s5_b_glu, ffn_w_gate_up, ffn_w_down):
    bsz, n_seq, d = x.shape
    seq = n_seq + FRONT
    half = d // 2
    n_blk = half // ATT_BLOCK
    scale = HEAD_DIM ** -0.5
    fox_heads = half // HEAD_DIM

    pad = jnp.zeros((bsz, N_PAD, d), x.dtype)
    meta = jnp.broadcast_to(meta_tokens[None].astype(x.dtype), (bsz, N_META, d))
    h = jnp.concatenate([pad, meta, x], axis=1)

    w_in = even_w_in[0]
    col = lambda n: w_in[:, n * half:(n + 1) * half]
    w_tok = jnp.concatenate([col(1), col(4)], axis=1).astype(BF16)
    w_tr = jnp.concatenate([col(0) * scale, col(2), col(3) * scale, col(5)], axis=1).T.astype(BF16)
    tok, tr = _norm_proj(h.reshape(bsz * seq, d), norm_mix_g[0][None, :], w_tok, w_tr)
    sb = _attention("sb", tok, tr, bsz, seq, 0, 0, n_blk, n_blk)
    lam_init = 0.8 - 0.6 * math.exp(-0.3 * 0)
    lam = (jnp.exp(jnp.sum(diff_lam_q1[0].astype(F32) * diff_lam_k1[0].astype(F32)))
           - jnp.exp(jnp.sum(diff_lam_q2[0].astype(F32) * diff_lam_k2[0].astype(F32))) + lam_init)
    g_col = jnp.broadcast_to(diff_subln_g[0].astype(F32)[:, None], (2 * HEAD_DIM, ATT_TILE))
    df = _attention("diff", tok, tr, bsz, seq, n_blk, 2 * n_blk, 3 * n_blk, n_blk,
                    (lam.reshape(1, 1).astype(F32), g_col), out_scale=1.0 - lam_init)
    h = _post_mixer(h, sb.reshape(bsz, seq, half), df.reshape(bsz, seq, half),
                    even_w_out[0].astype(BF16), norm_ffn_g[0][None, :],
                    ffn_w_gate_up[0].astype(BF16), ffn_w_down[0].astype(BF16))

    w_in = odd_w_in[0]
    col = lambda n: w_in[:, n * half:(n + 1) * half]
    w_tok = col(1).astype(BF16)
    w_tr = jnp.concatenate([col(0) * scale, col(2)], axis=1).T.astype(BF16)
    w_f = w_in[:, 3 * half:3 * half + fox_heads]
    w_u = w_in[:, 3 * half + fox_heads:].astype(BF16)
    f_rows = BF16_ROWS
    w_ft = jnp.zeros((f_rows, d), F32).at[:fox_heads].set(w_f.T)
    tok, tr, u, ft = _norm_proj(h.reshape(bsz * seq, d), norm_mix_g[1][None, :], w_tok, w_tr,
                                odd_weights=(w_u,) + _split_bf16(w_ft))
    ft = jnp.transpose(ft.reshape(f_rows, bsz, seq), (1, 0, 2))
    b_f = jnp.zeros((f_rows, 1), F32).at[:fox_heads, 0].set(fox_b_f[0].astype(F32))
    f_row = _forget_cumsum(ft, b_f)[:, :fox_heads]
    fox = _attention("fox", tok, tr, bsz, seq, 0, 0, n_blk, n_blk,
                     _forget_bias_operands(f_row, n_blk))
    ops = _s5_operators(s5_lam_re[0], s5_lam_im[0], s5_log_dt[0], s5_b_re[0], s5_b_im[0],
                        s5_c_re[0], s5_c_im[0], s5_d[0])
    ssm = _s5_layer(u.reshape(bsz, seq, -1), ops)
    h = _post_mixer(h, fox.reshape(bsz, seq, half), ssm, odd_w_out[0].astype(BF16),
                    norm_ffn_g[1][None, :], ffn_w_gate_up[1].astype(BF16),
                    ffn_w_down[1].astype(BF16),
                    glu=(s5_w_glu[0].astype(BF16), s5_b_glu[0][None, :].astype(F32)),
                    final_g=final_norm_g[None, :])
    return h[:, FRONT:]
```

```python
import functools
import math

import jax
import jax.numpy as jnp
from jax import lax
from jax.experimental import pallas as pl
from jax.experimental.pallas import tpu as pltpu

F32 = jnp.float32
BF16 = jnp.bfloat16

HEAD_DIM = 64
LANES = 128
ATT_BLOCK = 256
ATT_HEADS = ATT_BLOCK // HEAD_DIM
BF16_ROWS = 16
N_META = 16
ATT_TILE = 256
ATT_KTILE = 768
SB_BLOCK = 128
FRONT = ATT_TILE
N_PAD = FRONT - N_META
NEG = -1e30
RMS_EPS = 1e-6
SB_DEAD_LOG = -760.0
S5_GROUP = 16
S5_STATE = 64
S5_T = 8
S5_BLOCK_GROUPS = LANES // S5_GROUP
VMEM_LIMIT = 56 * 1024 * 1024


def _cparams(n_axes):
    return pltpu.CompilerParams(
        dimension_semantics=("arbitrary",) * n_axes, vmem_limit_bytes=VMEM_LIMIT)


def _resident(shape, index_map):
    return pl.BlockSpec(shape, index_map, pipeline_mode=pl.Buffered(1))


def _trunc_bf16(x):
    bits = lax.bitcast_convert_type(x, jnp.uint32) & jnp.uint32(0xFFFF0000)
    return lax.bitcast_convert_type(bits, F32)


def _split_bf16(w):
    hi = _trunc_bf16(w)
    return hi.astype(BF16), (w - hi).astype(BF16)


def _dot(a, b):
    return jnp.dot(a, b, preferred_element_type=F32)


def _dot_nt(a, b):
    return lax.dot_general(a, b, (((1,), (1,)), ((), ())), preferred_element_type=F32)


def _dot3(a_hi, a_lo, b_hi, b_lo):
    return _dot(a_hi, b_hi) + _dot(a_lo, b_hi) + _dot(a_hi, b_lo)


def _rms(x, g):
    ms = jnp.mean(x * x, axis=-1, keepdims=True)
    return x * lax.rsqrt(ms + RMS_EPS) * g


def _proj_kernel(*refs, n_chunk, odd):
    if odd:
        (x_ref, g_ref, wtok_ref, wtr_ref, wu_ref, wfh_ref, wfl_ref,
         tok_ref, tr_ref, u_ref, ft_ref) = refs
    else:
        x_ref, g_ref, wtok_ref, wtr_ref, tok_ref, tr_ref = refs
    hn32 = _rms(x_ref[...], g_ref[...])
    hn = hn32.astype(BF16)
    for c in range(0, tok_ref.shape[-1], n_chunk):
        tok_ref[:, c:c + n_chunk] = _dot(hn, wtok_ref[:, c:c + n_chunk]).astype(tok_ref.dtype)
    for c in range(0, tr_ref.shape[0], n_chunk):
        tr_ref[c:c + n_chunk, :] = _dot_nt(wtr_ref[c:c + n_chunk, :], hn).astype(tr_ref.dtype)
    if odd:
        u_ref[...] = _dot(hn, wu_ref[...])
        hn_lo = (hn32 - hn.astype(F32)).astype(BF16)
        wfh = wfh_ref[...]
        ft_ref[...] = _dot_nt(wfh, hn) + _dot_nt(wfh, hn_lo) + _dot_nt(wfl_ref[...], hn)


def _norm_proj(h, g, w_tok, w_tr, odd_weights=None):
    m, d = h.shape
    tm = next(t for t in (512, 256) if m % t == 0)
    n_tok, n_tr = w_tok.shape[1], w_tr.shape[0]
    const = lambda shape: _resident(shape, lambda i: (0, 0))
    args = [h, g, w_tok, w_tr]
    in_specs = [pl.BlockSpec((tm, d), lambda i: (i, 0)), const((1, d)),
                const((d, n_tok)), const((n_tr, d))]
    out_shape = [jax.ShapeDtypeStruct((m, n_tok), BF16), jax.ShapeDtypeStruct((n_tr, m), BF16)]
    out_specs = [pl.BlockSpec((tm, n_tok), lambda i: (i, 0)),
                 pl.BlockSpec((n_tr, tm), lambda i: (0, i))]
    if odd_weights is not None:
        wu, wfh, wfl = odd_weights
        nu, nf = wu.shape[1], wfh.shape[0]
        args += [wu, wfh, wfl]
        in_specs += [const((d, nu)), const((nf, d)), const((nf, d))]
        out_shape += [jax.ShapeDtypeStruct((m, nu), F32), jax.ShapeDtypeStruct((nf, m), F32)]
        out_specs += [pl.BlockSpec((tm, nu), lambda i: (i, 0)),
                      pl.BlockSpec((nf, tm), lambda i: (0, i))]
    return pl.pallas_call(
        functools.partial(_proj_kernel, n_chunk=256, odd=odd_weights is not None),
        out_shape=tuple(out_shape),
        grid=(m // tm,),
        in_specs=in_specs,
        out_specs=tuple(out_specs),
        compiler_params=_cparams(1),
        name="norm_proj_odd" if odd_weights is not None else "norm_proj_even",
    )(*args)


def _head_rows(h):
    return slice(h * HEAD_DIM, (h + 1) * HEAD_DIM)


def _head_queries(qt):
    zero = jnp.zeros((HEAD_DIM, qt.shape[1]), qt.dtype)
    return [jnp.concatenate([qt[_head_rows(r)] if r == h else zero for r in range(ATT_HEADS)],
                            axis=0) for h in range(ATT_HEADS)]


def _tile_mask(kind, start, off, tq, tk, n_pad, strict):
    if kind == "mid":
        return None
    row = lax.broadcasted_iota(jnp.int32, (tk, 1), 0)
    if kind == "first":
        return row >= n_pad
    dist = (lax.broadcasted_iota(jnp.int32, (tk, tq), 0)
            - lax.broadcasted_iota(jnp.int32, (tk, tq), 1))
    causal = (dist < off) if strict else (dist <= off)
    return causal & (row + start >= n_pad)


def _pipelined_tiles(n, produce, consume):
    produce(0, 0)

    @pl.when(n > 0)
    def _():
        produce(1, 1)
        consume(0, 0, "first")

    def body(k, carry):
        for parity in range(2):
            @pl.when(k % 2 == parity)
            def _():
                produce(1 - parity, k + 1)
                consume(parity, k, "mid")
        return carry

    lax.fori_loop(1, n, body, 0)
    for parity in range(2):
        @pl.when(n % 2 == parity)
        def _():
            consume(parity, n, "diag")


def _sb_kernel(k_ref, qt_ref, vt_ref, tri_ref, o_ref, acc_ref, carry_ref, *, tq, tk, blk, n_pad):
    i = pl.program_id(2)
    ratio = tk // tq
    c_diag = i // ratio
    off = (i - c_diag * ratio) * tq
    n_blocks = tk // blk
    q_cat = jnp.concatenate(_head_queries(qt_ref[...]), axis=1)
    tri = tri_ref[...]
    acc_ref[...] = jnp.zeros_like(acc_ref)
    carry_ref[...] = jnp.zeros_like(carry_ref)

    def step(c, kind):
        start = pl.multiple_of(c * tk, tk)
        ks = k_ref[pl.ds(start, tk), :]
        vt = vt_ref[:, pl.ds(start, tk)]
        mask = _tile_mask(kind, start, off, tq, tk, n_pad, strict=True)
        logits = _dot(ks, q_cat)
        for h in range(ATT_HEADS):
            x = logits[:, h * tq:(h + 1) * tq]
            sp = jnp.maximum(x, 0.0) + jnp.log(1.0 + jnp.exp(-jnp.abs(x)))
            log_beta = x - sp
            if mask is not None:
                sp = jnp.where(mask, sp, 0.0)
            hi = sp.astype(BF16)
            lo = (sp - hi.astype(F32)).astype(BF16)
            blocks = [slice(b * blk, (b + 1) * blk) for b in range(n_blocks)]
            within = _dot(tri, jnp.concatenate(
                [jnp.concatenate([hi[r], lo[r]], axis=0) for r in blocks], axis=1))
            run = carry_ref[h]
            w_blocks = [None] * n_blocks
            for b in reversed(range(n_blocks)):
                r = blocks[b]
                w_blocks[b] = jnp.exp(log_beta[r] + within[:, b * tq:(b + 1) * tq] + run)
                run = run - jnp.sum(sp[r], axis=0, keepdims=True)
            carry_ref[h] = run
            w = jnp.concatenate(w_blocks, axis=0)
            if mask is not None:
                w = jnp.where(mask, w, 0.0)
            rows = _head_rows(h)
            acc_ref[rows, :] += _dot(vt[rows, :], w.astype(BF16))

    def live():
        return (jnp.max(carry_ref[...]) > SB_DEAD_LOG).astype(jnp.int32)

    step(c_diag, "diag")

    def mid(state):
        n, _ = state
        step(c_diag - n, "mid")
        return n + 1, live()

    _, alive = lax.while_loop(lambda s: (s[0] < c_diag) & (s[1] > 0), mid, (jnp.int32(1), live()))

    @pl.when((c_diag > 0) & (alive > 0))
    def _():
        step(0, "first")

    o_ref[...] = acc_ref[...].T.astype(o_ref.dtype)


def _fox_queries(qt, qb):
    tq = qt.shape[1]
    zeros = lambda n: jnp.zeros((n, tq), qt.dtype)
    out = []
    for h in range(ATT_HEADS):
        own = [qt[_head_rows(h)], zeros(HEAD_DIM)]
        pieces = (own if h % 2 == 0 else own[::-1]) + [
            zeros(h * BF16_ROWS), qb[h * BF16_ROWS:(h + 1) * BF16_ROWS],
            zeros(2 * HEAD_DIM - (h + 1) * BF16_ROWS)]
        out.append(jnp.concatenate([p for p in pieces if p.shape[0]], axis=0))
    return out


def _softmax_attn_kernel(*refs, tq, tk, n_pad, mode, out_scale):
    if mode == "fox":
        k_ref, kb_ref, qt_ref, qb_ref, vt_ref, o_ref, acc_ref, m_ref, sc_ref = refs
        q_heads = _fox_queries(qt_ref[...], qb_ref[...])
        q_cats = [jnp.concatenate(q_heads[2 * p:2 * p + 2], axis=1) for p in range(ATT_HEADS // 2)]
    else:
        k_ref, qt_ref, vt_ref, lam_ref, g_ref, o_ref, acc_ref, m_ref, sc_ref = refs
        q_cats = [jnp.concatenate(_head_queries(qt_ref[...]), axis=1)]
    i = pl.program_id(2)
    ratio = tk // tq
    c_diag = i // ratio
    off = (i - c_diag * ratio) * tq
    ones_rows = (lax.broadcasted_iota(jnp.int32, (BF16_ROWS, tk), 0) == 0).astype(BF16)
    v_rows = HEAD_DIM if mode == "fox" else 2 * HEAD_DIM
    acc_ref[...] = jnp.zeros_like(acc_ref)
    m_ref[...] = jnp.full_like(m_ref, NEG)

    def produce(slot, c):
        rows = pl.ds(pl.multiple_of(c * tk, tk), tk)
        ks = k_ref[rows, :]
        if mode == "fox":
            kb = kb_ref[rows, :]
            k_ops = [jnp.concatenate([ks[:, p * LANES:(p + 1) * LANES], kb], axis=1)
                     for p in range(ATT_HEADS // 2)]
        else:
            k_ops = [ks]
        width = ATT_HEADS * tq // len(q_cats)
        for n, (k_op, q) in enumerate(zip(k_ops, q_cats)):
            sc_ref[slot, :, n * width:(n + 1) * width] = _dot(k_op, q)

    def consume(slot, c, kind):
        start = pl.multiple_of(c * tk, tk)
        vt = vt_ref[:, pl.ds(start, tk)]
        mask = _tile_mask(kind, start, off, tq, tk, n_pad, strict=False)
        for h in range(ATT_HEADS):
            s = sc_ref[slot, :, h * tq:(h + 1) * tq]
            if mask is not None:
                s = jnp.where(mask, s, NEG)
            m_old = m_ref[h]
            m_new = jnp.maximum(m_old, jnp.max(s, axis=0, keepdims=True))
            alpha = jnp.exp(m_old - m_new)
            p = jnp.exp(s - m_new).astype(BF16)
            m_ref[h] = m_new
            vh = vt[_head_rows(h)] if mode == "fox" else vt[(h // 2) * v_rows:(h // 2 + 1) * v_rows]
            v_aug = jnp.concatenate([vh, ones_rows], axis=0)
            acc_ref[h] = alpha * acc_ref[h] + _dot(v_aug, p)

    _pipelined_tiles(c_diag, produce, consume)

    outs = []
    for h in range(ATT_HEADS):
        acc = acc_ref[h]
        outs.append(acc[:v_rows] * (1.0 / acc[v_rows:v_rows + 1]))
    if mode == "diff":
        heads = []
        for d in range(ATT_HEADS // 2):
            out_d = outs[2 * d] - lam_ref[0, 0] * outs[2 * d + 1]
            ms = jnp.mean(out_d * out_d, axis=0, keepdims=True)
            heads.append(out_d * lax.rsqrt(ms + RMS_EPS) * g_ref[...] * out_scale)
        outs = heads
    o_ref[...] = jnp.concatenate(outs, axis=0).T.astype(o_ref.dtype)


def _attention(mode, tok, tr, bsz, seq, k_blk, q_blk, v_blk, n_blk, extra=(), out_scale=1.0):
    tq, tk = ATT_TILE, ATT_KTILE
    nq = seq // tq
    k_spec = pl.BlockSpec((seq, ATT_BLOCK), lambda b, p, i: (b, k_blk + p))
    qt_spec = pl.BlockSpec((ATT_BLOCK, tq), lambda b, p, i: (q_blk + p, b * nq + i))
    vt_spec = pl.BlockSpec((ATT_BLOCK, seq), lambda b, p, i: (v_blk + p, b))
    m_scratch = pltpu.VMEM((ATT_HEADS, 1, tq), F32)
    sc_scratch = pltpu.VMEM((2, tk, ATT_HEADS * tq), F32)
    if mode == "sb":
        idx = jnp.arange(SB_BLOCK)
        tri = -(idx[None, :] > idx[:, None]).astype(BF16)
        args = [tok, tr, tr, jnp.concatenate([tri, tri], axis=1)]
        in_specs = [k_spec, qt_spec, vt_spec,
                    _resident((SB_BLOCK, 2 * SB_BLOCK), lambda b, p, i: (0, 0))]
        scratch = [pltpu.VMEM((ATT_BLOCK, tq), F32), m_scratch]
        body = functools.partial(_sb_kernel, tq=tq, tk=tk, blk=SB_BLOCK, n_pad=N_PAD)
    elif mode == "fox":
        k_bias, q_bias = extra
        args = [tok, k_bias, tr, q_bias, tr]
        in_specs = [k_spec, pl.BlockSpec((seq, LANES), lambda b, p, i: (b, p)), qt_spec,
                    pl.BlockSpec((None, None, ATT_HEADS * BF16_ROWS, tq), lambda b, p, i: (b, p, 0, i)),
                    vt_spec]
        scratch = [pltpu.VMEM((ATT_HEADS, HEAD_DIM + BF16_ROWS, tq), F32), m_scratch, sc_scratch]
        body = functools.partial(_softmax_attn_kernel, tq=tq, tk=tk, n_pad=N_PAD, mode=mode,
                                 out_scale=out_scale)
    else:
        lam, g_col = extra
        args = [tok, tr, tr, lam, g_col]
        in_specs = [k_spec, qt_spec, vt_spec, pl.BlockSpec(memory_space=pltpu.SMEM),
                    _resident((2 * HEAD_DIM, tq), lambda b, p, i: (0, 0))]
        scratch = [pltpu.VMEM((ATT_HEADS, 2 * HEAD_DIM + BF16_ROWS, tq), F32), m_scratch, sc_scratch]
        body = functools.partial(_softmax_attn_kernel, tq=tq, tk=tk, n_pad=N_PAD, mode=mode,
                                 out_scale=out_scale)
    return pl.pallas_call(
        body,
        out_shape=jax.ShapeDtypeStruct((bsz * seq, n_blk * ATT_BLOCK), BF16),
        grid=(bsz, n_blk, nq),
        in_specs=in_specs,
        out_specs=pl.BlockSpec((tq, ATT_BLOCK), lambda b, p, i: (b * nq + i, p)),
        scratch_shapes=scratch,
        compiler_params=_cparams(3),
        name=mode + "_attention",
    )(*args)


def _split3_bf16(x):
    hi = x.astype(BF16)
    r1 = x - hi.astype(F32)
    mid = r1.astype(BF16)
    lo = (r1 - mid.astype(F32)).astype(BF16)
    return hi, mid, lo


def _fcum_kernel(ft_ref, b_ref, tri_ref, selq_ref, selk_ref, qb_ref, kb_ref, *, tile, n_pad):
    rows, seq = ft_ref.shape
    tri = tri_ref[...]
    one_row = (lax.broadcasted_iota(jnp.int32, (rows, tile), 0) == 0).astype(BF16)
    carry = jnp.zeros((rows, 1), F32)
    for c in range(seq // tile):
        cols = slice(c * tile, (c + 1) * tile)
        x = ft_ref[:, cols] + b_ref[...]
        log_f = jnp.minimum(x, 0.0) - jnp.log(1.0 + jnp.exp(-jnp.abs(x)))
        pos = lax.broadcasted_iota(jnp.int32, (1, tile), 1) + c * tile
        log_f = jnp.where(pos >= n_pad, log_f, 0.0)
        f_cum = _dot(jnp.concatenate(_split3_bf16(log_f), axis=1), tri) + carry
        carry = carry + jnp.sum(log_f, axis=1, keepdims=True)
        terms = jnp.concatenate(_split3_bf16(f_cum) + (one_row,), axis=0).astype(F32)
        qb_ref[:, cols] = _dot(selq_ref[...], terms).astype(qb_ref.dtype)
        kb_ref[cols, :] = _dot(selk_ref[...], terms).T.astype(kb_ref.dtype)


def _forget_bias_operands(ft, b_f, heads, n_blk):
    bsz, rows, seq = ft.shape
    tile = ATT_TILE
    idx = jnp.arange(tile)
    tri = (idx[:, None] <= idx[None, :]).astype(BF16)
    tri3 = jnp.concatenate([tri, tri, tri], axis=0)
    h = jnp.arange(heads)
    q_row = h * BF16_ROWS
    k_lane = (h // ATT_HEADS) * LANES + (h % ATT_HEADS) * BF16_ROWS
    selq = jnp.zeros((heads * BF16_ROWS, 4 * rows), F32)
    selk = jnp.zeros((n_blk * LANES, 4 * rows), F32)
    for term in range(3):
        selq = selq.at[q_row + 3 + term, term * rows + h].set(1.0)
        selq = selq.at[q_row + term, 3 * rows].set(-1.0)
        selk = selk.at[k_lane + term, term * rows + h].set(1.0)
        selk = selk.at[k_lane + 3 + term, 3 * rows].set(1.0)
    const = lambda shape: _resident(shape, lambda b: (0, 0))
    q_bias, k_bias = pl.pallas_call(
        functools.partial(_fcum_kernel, tile=tile, n_pad=N_PAD),
        out_shape=(jax.ShapeDtypeStruct((bsz, heads * BF16_ROWS, seq), BF16),
                   jax.ShapeDtypeStruct((bsz, seq, n_blk * LANES), BF16)),
        grid=(bsz,),
        in_specs=[pl.BlockSpec((None, rows, seq), lambda b: (b, 0, 0)),
                  const((rows, 1)), const((3 * tile, tile)),
                  const(selq.shape), const(selk.shape)],
        out_specs=(pl.BlockSpec((None, heads * BF16_ROWS, seq), lambda b: (b, 0, 0)),
                   pl.BlockSpec((None, seq, n_blk * LANES), lambda b: (b, 0, 0))),
        compiler_params=_cparams(1),
        name="forget_cumsum",
    )(ft, b_f, tri3, selq, selk)
    return (k_bias.reshape(bsz * seq, n_blk * LANES),
            q_bias.reshape(bsz, n_blk, ATT_HEADS * BF16_ROWS, seq))


def _s5_kernel(u_ref, toep_h, toep_l, sb_h, sb_l, at_ref, ca_h, ca_l, y_ref, acc_ref, st_ref,
               *, t, row_chunk):
    n_chunks = st_ref.shape[0]
    half = st_ref.shape[1] // 2

    def split(x):
        hi = x.astype(BF16)
        return hi, (x - hi.astype(F32)).astype(BF16)

    def chunk_rows(ref, r, step):
        return ref.at[pl.ds(r * row_chunk * t + step, row_chunk, stride=t), :]

    def stage1(r, c):
        ut = jnp.concatenate([chunk_rows(u_ref, r, step)[...] for step in range(t)], axis=1)
        uh, ul = split(ut)
        sl = pl.ds(pl.multiple_of(r * row_chunk, 8), row_chunk)
        acc_ref[sl, :] = _dot3(uh, ul, toep_h[...], toep_l[...])
        st_ref[sl, :] = _dot3(uh, ul, sb_h[...], sb_l[...])
        return c

    lax.fori_loop(0, n_chunks // row_chunk, stage1, 0)

    a_re = at_ref[0:1, :]
    a_im = at_ref[1:2, :]

    def scan(c, state):
        x_re, x_im = state
        row = pl.ds(c, 1)
        s_re = st_ref[row, :half]
        s_im = st_ref[row, half:]
        st_ref[row, :half] = x_re
        st_ref[row, half:] = x_im
        return (a_re * x_re - a_im * x_im + s_re, a_re * x_im + a_im * x_re + s_im)

    zero = jnp.zeros((1, half), F32)
    lax.fori_loop(0, n_chunks, scan, (zero, zero), unroll=4)

    def stage3(r, c):
        sl = pl.ds(pl.multiple_of(r * row_chunk, 8), row_chunk)
        xh, xl = split(st_ref[sl, :])
        y = acc_ref[sl, :] + _dot3(xh, xl, ca_h[...], ca_l[...])
        y = 0.5 * y * (1.0 + jnp.tanh(math.sqrt(2.0 / math.pi) * (y + 0.044715 * (y * y * y))))
        for step in range(t):
            chunk_rows(y_ref, r, step)[...] = y[:, step * LANES:(step + 1) * LANES]
        return c

    lax.fori_loop(0, n_chunks // row_chunk, stage3, 0)


def _s5_operators(lam_re, lam_im, log_dt, b_re, b_im, c_re, c_im, d_skip):
    t = S5_T
    lre, lim = lam_re.astype(F32), lam_im.astype(F32)
    dt = jnp.exp(log_dt.astype(F32))[:, None]
    mag = jnp.exp(lre * dt)
    a_re, a_im = mag * jnp.cos(lim * dt), mag * jnp.sin(lim * dt)
    den = lre * lre + lim * lim
    g_re = ((a_re - 1.0) * lre + a_im * lim) / den
    g_im = (a_im * lre - (a_re - 1.0) * lim) / den
    br, bi = b_re.astype(F32), b_im.astype(F32)
    bb_re = g_re[..., None] * br - g_im[..., None] * bi
    bb_im = g_re[..., None] * bi + g_im[..., None] * br
    k = jnp.arange(t + 1, dtype=F32)[:, None, None]
    pw_mag = jnp.exp(k * (lre * dt)[None])
    pw_re = pw_mag * jnp.cos(k * (lim * dt)[None])
    pw_im = pw_mag * jnp.sin(k * (lim * dt)[None])
    cr, ci = c_re.astype(F32), c_im.astype(F32)
    hp = lax.Precision.HIGHEST
    ab_re = pw_re[..., None] * bb_re[None] - pw_im[..., None] * bb_im[None]
    ab_im = pw_re[..., None] * bb_im[None] + pw_im[..., None] * bb_re[None]
    kern = (jnp.einsum('ghp,kgpe->kghe', cr, ab_re, precision=hp)
            - jnp.einsum('ghp,kgpe->kghe', ci, ab_im, precision=hp))
    lag = jnp.arange(t)[None, :] - jnp.arange(t)[:, None]
    toep = jnp.where((lag >= 0)[:, :, None, None, None],
                     kern[jnp.clip(lag, 0, t)], 0.0)
    eye_t = jnp.eye(t, dtype=F32)
    eye_h = jnp.eye(S5_GROUP, dtype=F32)
    toep = toep + (eye_t[:, :, None, None, None] * eye_h[None, None, None]
                   * d_skip.astype(F32)[None, None, :, :, None])
    g_n = toep.shape[2]
    nb, gb = g_n // S5_BLOCK_GROUPS, S5_BLOCK_GROUPS
    eye_g = jnp.eye(gb, dtype=F32)
    width = t * gb * S5_GROUP
    toep = jnp.transpose(toep, (2, 0, 4, 1, 3)).reshape(nb, gb, t, S5_GROUP, t, S5_GROUP)
    toep = jnp.einsum('bgjaih,gk->bjgaikh', toep, eye_g).reshape(nb, width, width)
    rev = t - 1 - jnp.arange(t)

    def state_in(ab):
        x = jnp.transpose(ab[rev], (1, 0, 3, 2)).reshape(nb, gb, t, S5_GROUP, S5_STATE)
        return jnp.einsum('bgjap,gk->bjgakp', x, eye_g).reshape(nb, width, gb * S5_STATE)

    sb = jnp.concatenate([state_in(ab_re), state_in(ab_im)], axis=2)
    e_re, e_im = pw_re[1:], pw_im[1:]
    ca_re = cr[None] * e_re[:, :, None, :] - ci[None] * e_im[:, :, None, :]
    ca_im = -(cr[None] * e_im[:, :, None, :] + ci[None] * e_re[:, :, None, :])

    def state_out(ca):
        x = jnp.transpose(ca, (1, 3, 0, 2)).reshape(nb, gb, S5_STATE, t, S5_GROUP)
        return jnp.einsum('bgpih,gk->bgpikh', x, eye_g).reshape(nb, gb * S5_STATE, width)

    ca = jnp.concatenate([state_out(ca_re), state_out(ca_im)], axis=1)
    at = jnp.stack([pw_re[t].reshape(nb, gb * S5_STATE), pw_im[t].reshape(nb, gb * S5_STATE)], axis=1)
    return [*_split_bf16(toep), *_split_bf16(sb), at, *_split_bf16(ca)]


def _s5_layer(u, ops):
    bsz, seq, ch = u.shape
    t = S5_T
    n_chunks = seq // t
    width = t * LANES
    n_state = 2 * S5_BLOCK_GROUPS * S5_STATE
    row_chunk = next(n_chunks // n for n in (8, 4, 2, 1) if n_chunks % (8 * n) == 0)
    op_spec = lambda r, c: pl.BlockSpec((None, r, c), lambda p, b: (p, 0, 0),
                                        pipeline_mode=pl.Buffered(1))
    slab = pl.BlockSpec((None, seq, LANES), lambda p, b: (b, 0, p))
    return pl.pallas_call(
        functools.partial(_s5_kernel, t=t, row_chunk=row_chunk),
        out_shape=jax.ShapeDtypeStruct((bsz, seq, ch), F32),
        grid=(ch // LANES, bsz),
        in_specs=[slab, op_spec(width, width), op_spec(width, width),
                  op_spec(width, n_state), op_spec(width, n_state), op_spec(2, n_state // 2),
                  op_spec(n_state, width), op_spec(n_state, width)],
        out_specs=slab,
        scratch_shapes=[pltpu.VMEM((n_chunks, width), F32), pltpu.VMEM((n_chunks, n_state), F32)],
        compiler_params=_cparams(2),
        name="s5_chunked",
    )(u, *ops)


def _post_kernel(*refs, n_pad, ff_chunk, glu, final):
    h_ref, a_ref, b_ref = refs[:3]
    pos = 3
    if glu:
        wglu_ref, bglu_ref = refs[pos:pos + 2]
        pos += 2
    wo_ref, gf_ref, wgu_ref, wd_ref = refs[pos:pos + 4]
    pos += 4
    if final:
        gl_ref = refs[pos]
        pos += 1
    o_ref, acc_ref = refs[pos:pos + 2]

    tm = h_ref.shape[0]
    half = a_ref.shape[-1]
    d_ff = wd_ref.shape[0]
    row = lax.broadcasted_iota(jnp.int32, (tm, 1), 0) + pl.program_id(1) * tm
    valid = row >= n_pad

    a = a_ref[...]
    if glu:
        y = b_ref[...]
        z = _dot(y.astype(BF16), wglu_ref[...]) + bglu_ref[...]
        b = (y * (1.0 / (1.0 + jnp.exp(-z)))).astype(BF16)
    else:
        b = b_ref[...]
    mix = _dot(a, wo_ref[:half, :]) + _dot(b, wo_ref[half:, :])
    h1 = jnp.where(valid, h_ref[...] + mix, 0.0)
    hn = _rms(h1, gf_ref[...]).astype(BF16)
    acc_ref[...] = jnp.zeros_like(acc_ref)
    for c in range(0, d_ff, ff_chunk):
        gate = _dot(hn, wgu_ref[:, c:c + ff_chunk])
        up = _dot(hn, wgu_ref[:, d_ff + c:d_ff + c + ff_chunk])
        act = (gate * (1.0 / (1.0 + jnp.exp(-gate))) * up).astype(BF16)
        acc_ref[...] += _dot(act, wd_ref[c:c + ff_chunk, :])
    h2 = jnp.where(valid, h1 + acc_ref[...], 0.0)
    if final:
        h2 = _rms(h2, gl_ref[...])
    o_ref[...] = h2


def _post_mixer(h, a, b, w_out, g_ffn, w_gu, w_down, glu=None, final_g=None):
    bsz, seq, d = h.shape
    tm = next(t for t in (768, 512, 256) if seq % t == 0)
    half = a.shape[-1]
    d_ff = w_down.shape[0]
    row_spec = lambda w: pl.BlockSpec((None, tm, w), lambda bi, i: (bi, i, 0))
    const = lambda shape: _resident(shape, lambda bi, i: (0,) * len(shape))
    args = [h, a, b]
    in_specs = [row_spec(d), row_spec(half), row_spec(half)]
    if glu is not None:
        args += list(glu)
        in_specs += [const((half, half)), const((1, half))]
    args += [w_out, g_ffn, w_gu, w_down]
    in_specs += [const((2 * half, d)), const((1, d)), const((d, 2 * d_ff)), const((d_ff, d))]
    if final_g is not None:
        args.append(final_g)
        in_specs.append(const((1, d)))
    return pl.pallas_call(
        functools.partial(_post_kernel, n_pad=N_PAD, ff_chunk=256,
                          glu=glu is not None, final=final_g is not None),
        out_shape=jax.ShapeDtypeStruct((bsz, seq, d), F32),
        grid=(bsz, seq // tm),
        in_specs=in_specs,
        out_specs=row_spec(d),
        scratch_shapes=[pltpu.VMEM((tm, d), F32)],
        compiler_params=_cparams(2),
        name="outproj_ffn",
    )(*args)


def kernel(x, meta_tokens, norm_mix_g, norm_ffn_g, final_norm_g, even_w_in, even_w_out,
           diff_lam_q1, diff_lam_k1, diff_lam_q2, diff_lam_k2, diff_subln_g, odd_w_in,
           odd_w_out, fox_b_f, s5_lam_re, s5_lam_im, s5_log_dt, s5_b_re, s5_b_im, s5_c_re,
           s5_c_im, s5_d, s5_w_glu, s5_b_glu, ffn_w_gate_up, ffn_w_down):
    bsz, n_seq, d = x.shape
    seq = n_seq + FRONT
    half = d // 2
    n_blk = half // ATT_BLOCK
    scale = HEAD_DIM ** -0.5
    fox_heads = half // HEAD_DIM

    pad = jnp.zeros((bsz, N_PAD, d), x.dtype)
    meta = jnp.broadcast_to(meta_tokens[None].astype(x.dtype), (bsz, N_META, d))
    h = jnp.concatenate([pad, meta, x], axis=1)

    w_in = even_w_in[0]
    col = lambda n: w_in[:, n * half:(n + 1) * half]
    w_tok = jnp.concatenate([col(1), col(4)], axis=1).astype(BF16)
    w_tr = jnp.concatenate([col(0) * scale, col(2), col(3) * scale, col(5)], axis=1).T.astype(BF16)
    tok, tr = _norm_proj(h.reshape(bsz * seq, d), norm_mix_g[0][None, :], w_tok, w_tr)
    sb = _attention("sb", tok, tr, bsz, seq, 0, 0, n_blk, n_blk)
    lam_init = 0.8 - 0.6 * math.exp(-0.3 * 0)
    lam = (jnp.exp(jnp.sum(diff_lam_q1[0].astype(F32) * diff_lam_k1[0].astype(F32)))
           - jnp.exp(jnp.sum(diff_lam_q2[0].astype(F32) * diff_lam_k2[0].astype(F32))) + lam_init)
    g_col = jnp.broadcast_to(diff_subln_g[0].astype(F32)[:, None], (2 * HEAD_DIM, ATT_TILE))
    df = _attention("diff", tok, tr, bsz, seq, n_blk, 2 * n_blk, 3 * n_blk, n_blk,
                    (lam.reshape(1, 1).astype(F32), g_col), out_scale=1.0 - lam_init)
    h = _post_mixer(h, sb.reshape(bsz, seq, half), df.reshape(bsz, seq, half),
                    even_w_out[0].astype(BF16), norm_ffn_g[0][None, :],
                    ffn_w_gate_up[0].astype(BF16), ffn_w_down[0].astype(BF16))

    w_in = odd_w_in[0]
    col = lambda n: w_in[:, n * half:(n + 1) * half]
    w_tok = col(1).astype(BF16)
    w_tr = jnp.concatenate([col(0) * scale, col(2)], axis=1).T.astype(BF16)
    w_f = w_in[:, 3 * half:3 * half + fox_heads]
    w_u = w_in[:, 3 * half + fox_heads:].astype(BF16)
    f_rows = BF16_ROWS
    w_ft = jnp.zeros((f_rows, d), F32).at[:fox_heads].set(w_f.T)
    tok, tr, u, ft = _norm_proj(h.reshape(bsz * seq, d), norm_mix_g[1][None, :], w_tok, w_tr,
                                odd_weights=(w_u,) + _split_bf16(w_ft))
    ft = jnp.transpose(ft.reshape(f_rows, bsz, seq), (1, 0, 2))
    b_f = jnp.zeros((f_rows, 1), F32).at[:fox_heads, 0].set(fox_b_f[0].astype(F32))
    fox = _attention("fox", tok, tr, bsz, seq, 0, 0, n_blk, n_blk,
                     _forget_bias_operands(ft, b_f, fox_heads, n_blk))
    ops = _s5_operators(s5_lam_re[0], s5_lam_im[0], s5_log_dt[0], s5_b_re[0], s5_b_im[0],
                        s5_c_re[0], s5_c_im[0], s5_d[0])
    ssm = _s5_layer(u.reshape(bsz, seq, -1), ops)
    h = _post_mixer(h, fox.reshape(bsz, seq, half), ssm, odd_w_out[0].astype(BF16),
                    norm_ffn_g[1][None, :], ffn_w_gate_up[1].astype(BF16),
                    ffn_w_down[1].astype(BF16),
                    glu=(s5_w_glu[0].astype(BF16), s5_b_glu[0][None, :].astype(F32)),
                    final_g=final_norm_g[None, :])
    return h[:, FRONT:]
```

```python
import functools
import math

import jax
import jax.numpy as jnp
from jax import lax
from jax.experimental import pallas as pl
from jax.experimental.pallas import tpu as pltpu

F32 = jnp.float32
BF16 = jnp.bfloat16

HEAD_DIM = 64
LANES = 128
ATT_BLOCK = 256
ATT_HEADS = ATT_BLOCK // HEAD_DIM
BF16_ROWS = 16
N_META = 16
ATT_TILE = 256
ATT_KTILE = 768
SB_BLOCK = 128
FRONT = ATT_TILE
N_PAD = FRONT - N_META
NEG = -1e30
RMS_EPS = 1e-6
SB_DEAD_LOG = -760.0
S5_GROUP = 16
S5_STATE = 64
S5_T = 8
S5_BLOCK_GROUPS = LANES // S5_GROUP
VMEM_LIMIT = 56 * 1024 * 1024


def _cparams(n_axes):
    return pltpu.CompilerParams(
        dimension_semantics=("arbitrary",) * n_axes, vmem_limit_bytes=VMEM_LIMIT)


def _resident(shape, index_map):
    return pl.BlockSpec(shape, index_map, pipeline_mode=pl.Buffered(1))


def _trunc_bf16(x):
    bits = lax.bitcast_convert_type(x, jnp.uint32) & jnp.uint32(0xFFFF0000)
    return lax.bitcast_convert_type(bits, F32)


def _split_bf16(w):
    hi = _trunc_bf16(w)
    return hi.astype(BF16), (w - hi).astype(BF16)


def _dot(a, b):
    return jnp.dot(a, b, preferred_element_type=F32)


def _dot_nt(a, b):
    return lax.dot_general(a, b, (((1,), (1,)), ((), ())), preferred_element_type=F32)


def _dot3(a_hi, a_lo, b_hi, b_lo):
    return _dot(a_hi, b_hi) + _dot(a_lo, b_hi) + _dot(a_hi, b_lo)


def _rms(x, g):
    ms = jnp.mean(x * x, axis=-1, keepdims=True)
    return x * lax.rsqrt(ms + RMS_EPS) * g


def _proj_kernel(*refs, n_chunk, odd):
    if odd:
        (x_ref, g_ref, wtok_ref, wtr_ref, wu_ref, wfh_ref, wfl_ref,
         tok_ref, tr_ref, u_ref, ft_ref) = refs
    else:
        x_ref, g_ref, wtok_ref, wtr_ref, tok_ref, tr_ref = refs
    hn32 = _rms(x_ref[...], g_ref[...])
    hn = hn32.astype(BF16)
    for c in range(0, tok_ref.shape[-1], n_chunk):
        tok_ref[:, c:c + n_chunk] = _dot(hn, wtok_ref[:, c:c + n_chunk]).astype(tok_ref.dtype)
    for c in range(0, tr_ref.shape[0], n_chunk):
        tr_ref[c:c + n_chunk, :] = _dot_nt(wtr_ref[c:c + n_chunk, :], hn).astype(tr_ref.dtype)
    if odd:
        u_ref[...] = _dot(hn, wu_ref[...])
        hn_lo = (hn32 - hn.astype(F32)).astype(BF16)
        wfh = wfh_ref[...]
        ft_ref[...] = _dot_nt(wfh, hn) + _dot_nt(wfh, hn_lo) + _dot_nt(wfl_ref[...], hn)


def _norm_proj(h, g, w_tok, w_tr, odd_weights=None):
    m, d = h.shape
    tm = next(t for t in (512, 256) if m % t == 0)
    n_tok, n_tr = w_tok.shape[1], w_tr.shape[0]
    const = lambda shape: _resident(shape, lambda i: (0, 0))
    args = [h, g, w_tok, w_tr]
    in_specs = [pl.BlockSpec((tm, d), lambda i: (i, 0)), const((1, d)),
                const((d, n_tok)), const((n_tr, d))]
    out_shape = [jax.ShapeDtypeStruct((m, n_tok), BF16), jax.ShapeDtypeStruct((n_tr, m), BF16)]
    out_specs = [pl.BlockSpec((tm, n_tok), lambda i: (i, 0)),
                 pl.BlockSpec((n_tr, tm), lambda i: (0, i))]
    if odd_weights is not None:
        wu, wfh, wfl = odd_weights
        nu, nf = wu.shape[1], wfh.shape[0]
        args += [wu, wfh, wfl]
        in_specs += [const((d, nu)), const((nf, d)), const((nf, d))]
        out_shape += [jax.ShapeDtypeStruct((m, nu), F32), jax.ShapeDtypeStruct((nf, m), F32)]
        out_specs += [pl.BlockSpec((tm, nu), lambda i: (i, 0)),
                      pl.BlockSpec((nf, tm), lambda i: (0, i))]
    return pl.pallas_call(
        functools.partial(_proj_kernel, n_chunk=256, odd=odd_weights is not None),
        out_shape=tuple(out_shape),
        grid=(m // tm,),
        in_specs=in_specs,
        out_specs=tuple(out_specs),
        compiler_params=_cparams(1),
        name="norm_proj_odd" if odd_weights is not None else "norm_proj_even",
    )(*args)


def _head_rows(h):
    return slice(h * HEAD_DIM, (h + 1) * HEAD_DIM)


def _head_queries(qt):
    zero = jnp.zeros((HEAD_DIM, qt.shape[1]), qt.dtype)
    return [jnp.concatenate([qt[_head_rows(r)] if r == h else zero for r in range(ATT_HEADS)],
                            axis=0) for h in range(ATT_HEADS)]


def _tile_mask(kind, start, off, tq, tk, n_pad, strict):
    if kind == "mid":
        return None
    row = lax.broadcasted_iota(jnp.int32, (tk, 1), 0)
    if kind == "first":
        return row >= n_pad
    dist = (lax.broadcasted_iota(jnp.int32, (tk, tq), 0)
            - lax.broadcasted_iota(jnp.int32, (tk, tq), 1))
    causal = (dist < off) if strict else (dist <= off)
    return causal & (row + start >= n_pad)


def _for_diag_extent(off, tq, tk, fn, cond=True):
    for rows in range(tq, tk + 1, tq):
        @pl.when(cond & (off == rows - tq))
        def _():
            fn(rows)


def _pipelined_tiles(n, off, tq, tk, produce, consume):
    produce(0, 0)

    @pl.when(n > 0)
    def _():
        produce(1, 1)
        consume(0, 0, "first", tk)

    def body(k, carry):
        for parity in range(2):
            @pl.when(k % 2 == parity)
            def _():
                produce(1 - parity, k + 1)
                consume(parity, k, "mid", tk)
        return carry

    lax.fori_loop(1, n, body, 0)
    for parity in range(2):
        _for_diag_extent(off, tq, tk, lambda rows: consume(parity, n, "diag", rows),
                         cond=n % 2 == parity)


def _sb_kernel(k_ref, qt_ref, vt_ref, tri_ref, o_ref, acc_ref, carry_ref, *, tq, tk, blk, n_pad):
    i = pl.program_id(2)
    ratio = tk // tq
    c_diag = i // ratio
    off = (i - c_diag * ratio) * tq
    q_cat = jnp.concatenate(_head_queries(qt_ref[...]), axis=1)
    tri = tri_ref[...]
    acc_ref[...] = jnp.zeros_like(acc_ref)
    carry_ref[...] = jnp.zeros_like(carry_ref)

    def step(c, kind, n_keys=tk):
        start = pl.multiple_of(c * tk, tk)
        ks = k_ref[pl.ds(start, n_keys), :]
        vt = vt_ref[:, pl.ds(start, n_keys)]
        mask = _tile_mask(kind, start, off, tq, n_keys, n_pad, strict=True)
        logits = _dot(ks, q_cat)
        n_blocks = n_keys // blk
        for h in range(ATT_HEADS):
            x = logits[:, h * tq:(h + 1) * tq]
            sp = jnp.maximum(x, 0.0) + jnp.log(1.0 + jnp.exp(-jnp.abs(x)))
            log_beta = x - sp
            if mask is not None:
                sp = jnp.where(mask, sp, 0.0)
            hi = sp.astype(BF16)
            lo = (sp - hi.astype(F32)).astype(BF16)
            blocks = [slice(b * blk, (b + 1) * blk) for b in range(n_blocks)]
            within = _dot(tri, jnp.concatenate(
                [jnp.concatenate([hi[r], lo[r]], axis=0) for r in blocks], axis=1))
            run = carry_ref[h]
            w_blocks = [None] * n_blocks
            for b in reversed(range(n_blocks)):
                r = blocks[b]
                w_blocks[b] = jnp.exp(log_beta[r] + within[:, b * tq:(b + 1) * tq] + run)
                run = run - jnp.sum(sp[r], axis=0, keepdims=True)
            carry_ref[h] = run
            w = jnp.concatenate(w_blocks, axis=0)
            if mask is not None:
                w = jnp.where(mask, w, 0.0)
            rows = _head_rows(h)
            acc_ref[rows, :] += _dot(vt[rows, :], w.astype(BF16))

    def live():
        return (jnp.max(carry_ref[...]) > SB_DEAD_LOG).astype(jnp.int32)

    _for_diag_extent(off, tq, tk, lambda n_keys: step(c_diag, "diag", n_keys))

    def mid(state):
        n, _ = state
        step(c_diag - n, "mid")
        return n + 1, live()

    _, alive = lax.while_loop(lambda s: (s[0] < c_diag) & (s[1] > 0), mid, (jnp.int32(1), live()))

    @pl.when((c_diag > 0) & (alive > 0))
    def _():
        step(0, "first")

    o_ref[...] = acc_ref[...].T.astype(o_ref.dtype)


def _fox_queries(qt, qb):
    tq = qt.shape[1]
    zeros = lambda n: jnp.zeros((n, tq), qt.dtype)
    out = []
    for h in range(ATT_HEADS):
        own = [qt[_head_rows(h)], zeros(HEAD_DIM)]
        pieces = (own if h % 2 == 0 else own[::-1]) + [
            zeros(h * BF16_ROWS), qb[h * BF16_ROWS:(h + 1) * BF16_ROWS],
            zeros(2 * HEAD_DIM - (h + 1) * BF16_ROWS)]
        out.append(jnp.concatenate([p for p in pieces if p.shape[0]], axis=0))
    return out


def _softmax_attn_kernel(*refs, tq, tk, n_pad, mode, out_scale):
    if mode == "fox":
        k_ref, kb_ref, qt_ref, qb_ref, vt_ref, o_ref, acc_ref, m_ref, sc_ref = refs
        q_heads = _fox_queries(qt_ref[...], qb_ref[...])
        q_cats = [jnp.concatenate(q_heads[2 * p:2 * p + 2], axis=1) for p in range(ATT_HEADS // 2)]
    else:
        k_ref, qt_ref, vt_ref, lam_ref, g_ref, o_ref, acc_ref, m_ref, sc_ref = refs
        q_cats = [jnp.concatenate(_head_queries(qt_ref[...]), axis=1)]
    i = pl.program_id(2)
    ratio = tk // tq
    c_diag = i // ratio
    off = (i - c_diag * ratio) * tq
    v_rows = HEAD_DIM if mode == "fox" else 2 * HEAD_DIM
    acc_ref[...] = jnp.zeros_like(acc_ref)
    m_ref[...] = jnp.full_like(m_ref, NEG)

    def produce(slot, c):
        rows = pl.ds(pl.multiple_of(c * tk, tk), tk)
        ks = k_ref[rows, :]
        if mode == "fox":
            kb = kb_ref[rows, :]
            k_ops = [jnp.concatenate([ks[:, p * LANES:(p + 1) * LANES], kb], axis=1)
                     for p in range(ATT_HEADS // 2)]
        else:
            k_ops = [ks]
        width = ATT_HEADS * tq // len(q_cats)
        for n, (k_op, q) in enumerate(zip(k_ops, q_cats)):
            sc_ref[slot, :, n * width:(n + 1) * width] = _dot(k_op, q)

    def consume(slot, c, kind, rows):
        start = pl.multiple_of(c * tk, tk)
        vt = vt_ref[:, pl.ds(start, rows)]
        mask = _tile_mask(kind, start, off, tq, rows, n_pad, strict=False)
        ones_rows = (lax.broadcasted_iota(jnp.int32, (BF16_ROWS, rows), 0) == 0).astype(BF16)
        for h in range(ATT_HEADS):
            s = sc_ref[slot, :rows, h * tq:(h + 1) * tq]
            if mask is not None:
                s = jnp.where(mask, s, NEG)
            m_old = m_ref[h]
            m_new = jnp.maximum(m_old, jnp.max(s, axis=0, keepdims=True))
            alpha = jnp.exp(m_old - m_new)
            p = jnp.exp(s - m_new).astype(BF16)
            m_ref[h] = m_new
            vh = vt[_head_rows(h)] if mode == "fox" else vt[(h // 2) * v_rows:(h // 2 + 1) * v_rows]
            v_aug = jnp.concatenate([vh, ones_rows], axis=0)
            acc_ref[h] = alpha * acc_ref[h] + _dot(v_aug, p)

    _pipelined_tiles(c_diag, off, tq, tk, produce, consume)

    outs = []
    for h in range(ATT_HEADS):
        acc = acc_ref[h]
        outs.append(acc[:v_rows] * (1.0 / acc[v_rows:v_rows + 1]))
    if mode == "diff":
        heads = []
        for d in range(ATT_HEADS // 2):
            out_d = outs[2 * d] - lam_ref[0, 0] * outs[2 * d + 1]
            ms = jnp.mean(out_d * out_d, axis=0, keepdims=True)
            heads.append(out_d * lax.rsqrt(ms + RMS_EPS) * g_ref[...] * out_scale)
        outs = heads
    o_ref[...] = jnp.concatenate(outs, axis=0).T.astype(o_ref.dtype)


def _attention(mode, tok, tr, bsz, seq, k_blk, q_blk, v_blk, n_blk, extra=(), out_scale=1.0):
    tq, tk = ATT_TILE, ATT_KTILE
    nq = seq // tq
    k_spec = pl.BlockSpec((seq, ATT_BLOCK), lambda b, p, i: (b, k_blk + p))
    qt_spec = pl.BlockSpec((ATT_BLOCK, tq), lambda b, p, i: (q_blk + p, b * nq + i))
    vt_spec = pl.BlockSpec((ATT_BLOCK, seq), lambda b, p, i: (v_blk + p, b))
    m_scratch = pltpu.VMEM((ATT_HEADS, 1, tq), F32)
    sc_scratch = pltpu.VMEM((2, tk, ATT_HEADS * tq), F32)
    if mode == "sb":
        idx = jnp.arange(SB_BLOCK)
        tri = -(idx[None, :] > idx[:, None]).astype(BF16)
        args = [tok, tr, tr, jnp.concatenate([tri, tri], axis=1)]
        in_specs = [k_spec, qt_spec, vt_spec,
                    _resident((SB_BLOCK, 2 * SB_BLOCK), lambda b, p, i: (0, 0))]
        scratch = [pltpu.VMEM((ATT_BLOCK, tq), F32), m_scratch]
        body = functools.partial(_sb_kernel, tq=tq, tk=tk, blk=SB_BLOCK, n_pad=N_PAD)
    elif mode == "fox":
        k_bias, q_bias = extra
        args = [tok, k_bias, tr, q_bias, tr]
        in_specs = [k_spec, pl.BlockSpec((seq, LANES), lambda b, p, i: (b, p)), qt_spec,
                    pl.BlockSpec((None, None, ATT_HEADS * BF16_ROWS, tq), lambda b, p, i: (b, p, 0, i)),
                    vt_spec]
        scratch = [pltpu.VMEM((ATT_HEADS, HEAD_DIM + BF16_ROWS, tq), F32), m_scratch, sc_scratch]
        body = functools.partial(_softmax_attn_kernel, tq=tq, tk=tk, n_pad=N_PAD, mode=mode,
                                 out_scale=out_scale)
    else:
        lam, g_col = extra
        args = [tok, tr, tr, lam, g_col]
        in_specs = [k_spec, qt_spec, vt_spec, pl.BlockSpec(memory_space=pltpu.SMEM),
                    _resident((2 * HEAD_DIM, tq), lambda b, p, i: (0, 0))]
        scratch = [pltpu.VMEM((ATT_HEADS, 2 * HEAD_DIM + BF16_ROWS, tq), F32), m_scratch, sc_scratch]
        body = functools.partial(_softmax_attn_kernel, tq=tq, tk=tk, n_pad=N_PAD, mode=mode,
                                 out_scale=out_scale)
    return pl.pallas_call(
        body,
        out_shape=jax.ShapeDtypeStruct((bsz * seq, n_blk * ATT_BLOCK), BF16),
        grid=(bsz, n_blk, nq),
        in_specs=in_specs,
        out_specs=pl.BlockSpec((tq, ATT_BLOCK), lambda b, p, i: (b * nq + i, p)),
        scratch_shapes=scratch,
        compiler_params=_cparams(3),
        name=mode + "_attention",
    )(*args)


def _split3_bf16(x):
    hi = x.astype(BF16)
    r1 = x - hi.astype(F32)
    mid = r1.astype(BF16)
    lo = (r1 - mid.astype(F32)).astype(BF16)
    return hi, mid, lo


def _fcum_kernel(ft_ref, b_ref, tri_ref, selq_ref, selk_ref, qb_ref, kb_ref, *, tile, n_pad):
    rows, seq = ft_ref.shape
    tri = tri_ref[...]
    one_row = (lax.broadcasted_iota(jnp.int32, (rows, tile), 0) == 0).astype(BF16)
    carry = jnp.zeros((rows, 1), F32)
    for c in range(seq // tile):
        cols = slice(c * tile, (c + 1) * tile)
        x = ft_ref[:, cols] + b_ref[...]
        log_f = jnp.minimum(x, 0.0) - jnp.log(1.0 + jnp.exp(-jnp.abs(x)))
        pos = lax.broadcasted_iota(jnp.int32, (1, tile), 1) + c * tile
        log_f = jnp.where(pos >= n_pad, log_f, 0.0)
        f_cum = _dot(jnp.concatenate(_split3_bf16(log_f), axis=1), tri) + carry
        carry = carry + jnp.sum(log_f, axis=1, keepdims=True)
        terms = jnp.concatenate(_split3_bf16(f_cum) + (one_row,), axis=0).astype(F32)
        qb_ref[:, cols] = _dot(selq_ref[...], terms).astype(qb_ref.dtype)
        kb_ref[cols, :] = _dot(selk_ref[...], terms).T.astype(kb_ref.dtype)


def _forget_bias_operands(ft, b_f, heads, n_blk):
    bsz, rows, seq = ft.shape
    tile = ATT_TILE
    idx = jnp.arange(tile)
    tri = (idx[:, None] <= idx[None, :]).astype(BF16)
    tri3 = jnp.concatenate([tri, tri, tri], axis=0)
    h = jnp.arange(heads)
    q_row = h * BF16_ROWS
    k_lane = (h // ATT_HEADS) * LANES + (h % ATT_HEADS) * BF16_ROWS
    selq = jnp.zeros((heads * BF16_ROWS, 4 * rows), F32)
    selk = jnp.zeros((n_blk * LANES, 4 * rows), F32)
    for term in range(3):
        selq = selq.at[q_row + 3 + term, term * rows + h].set(1.0)
        selq = selq.at[q_row + term, 3 * rows].set(-1.0)
        selk = selk.at[k_lane + term, term * rows + h].set(1.0)
        selk = selk.at[k_lane + 3 + term, 3 * rows].set(1.0)
    const = lambda shape: _resident(shape, lambda b: (0, 0))
    q_bias, k_bias = pl.pallas_call(
        functools.partial(_fcum_kernel, tile=tile, n_pad=N_PAD),
        out_shape=(jax.ShapeDtypeStruct((bsz, heads * BF16_ROWS, seq), BF16),
                   jax.ShapeDtypeStruct((bsz, seq, n_blk * LANES), BF16)),
        grid=(bsz,),
        in_specs=[pl.BlockSpec((None, rows, seq), lambda b: (b, 0, 0)),
                  const((rows, 1)), const((3 * tile, tile)),
                  const(selq.shape), const(selk.shape)],
        out_specs=(pl.BlockSpec((None, heads * BF16_ROWS, seq), lambda b: (b, 0, 0)),
                   pl.BlockSpec((None, seq, n_blk * LANES), lambda b: (b, 0, 0))),
        compiler_params=_cparams(1),
        name="forget_cumsum",
    )(ft, b_f, tri3, selq, selk)
    return (k_bias.reshape(bsz * seq, n_blk * LANES),
            q_bias.reshape(bsz, n_blk, ATT_HEADS * BF16_ROWS, seq))


def _s5_kernel(u_ref, toep_h, toep_l, sb_h, sb_l, at_ref, ca_h, ca_l, y_ref, acc_ref, st_ref,
               *, t, row_chunk):
    n_chunks = st_ref.shape[0]
    half = st_ref.shape[1] // 2

    def split(x):
        hi = x.astype(BF16)
        return hi, (x - hi.astype(F32)).astype(BF16)

    def chunk_rows(ref, r, step):
        return ref.at[pl.ds(r * row_chunk * t + step, row_chunk, stride=t), :]

    def stage1(r, c):
        ut = jnp.concatenate([chunk_rows(u_ref, r, step)[...] for step in range(t)], axis=1)
        uh, ul = split(ut)
        sl = pl.ds(pl.multiple_of(r * row_chunk, 8), row_chunk)
        acc_ref[sl, :] = _dot3(uh, ul, toep_h[...], toep_l[...])
        st_ref[sl, :] = _dot3(uh, ul, sb_h[...], sb_l[...])
        return c

    lax.fori_loop(0, n_chunks // row_chunk, stage1, 0)

    a_re = at_ref[0:1, :]
    a_im = at_ref[1:2, :]

    def scan(c, state):
        x_re, x_im = state
        row = pl.ds(c, 1)
        s_re = st_ref[row, :half]
        s_im = st_ref[row, half:]
        st_ref[row, :half] = x_re
        st_ref[row, half:] = x_im
        return (a_re * x_re - a_im * x_im + s_re, a_re * x_im + a_im * x_re + s_im)

    zero = jnp.zeros((1, half), F32)
    lax.fori_loop(0, n_chunks, scan, (zero, zero), unroll=4)

    def stage3(r, c):
        sl = pl.ds(pl.multiple_of(r * row_chunk, 8), row_chunk)
        xh, xl = split(st_ref[sl, :])
        y = acc_ref[sl, :] + _dot3(xh, xl, ca_h[...], ca_l[...])
        y = 0.5 * y * (1.0 + jnp.tanh(math.sqrt(2.0 / math.pi) * (y + 0.044715 * (y * y * y))))
        for step in range(t):
            chunk_rows(y_ref, r, step)[...] = y[:, step * LANES:(step + 1) * LANES]
        return c

    lax.fori_loop(0, n_chunks // row_chunk, stage3, 0)


def _s5_operators(lam_re, lam_im, log_dt, b_re, b_im, c_re, c_im, d_skip):
    t = S5_T
    lre, lim = lam_re.astype(F32), lam_im.astype(F32)
    dt = jnp.exp(log_dt.astype(F32))[:, None]
    mag = jnp.exp(lre * dt)
    a_re, a_im = mag * jnp.cos(lim * dt), mag * jnp.sin(lim * dt)
    den = lre * lre + lim * lim
    g_re = ((a_re - 1.0) * lre + a_im * lim) / den
    g_im = (a_im * lre - (a_re - 1.0) * lim) / den
    br, bi = b_re.astype(F32), b_im.astype(F32)
    bb_re = g_re[..., None] * br - g_im[..., None] * bi
    bb_im = g_re[..., None] * bi + g_im[..., None] * br
    k = jnp.arange(t + 1, dtype=F32)[:, None, None]
    pw_mag = jnp.exp(k * (lre * dt)[None])
    pw_re = pw_mag * jnp.cos(k * (lim * dt)[None])
    pw_im = pw_mag * jnp.sin(k * (lim * dt)[None])
    cr, ci = c_re.astype(F32), c_im.astype(F32)
    hp = lax.Precision.HIGHEST
    ab_re = pw_re[..., None] * bb_re[None] - pw_im[..., None] * bb_im[None]
    ab_im = pw_re[..., None] * bb_im[None] + pw_im[..., None] * bb_re[None]
    kern = (jnp.einsum('ghp,kgpe->kghe', cr, ab_re, precision=hp)
            - jnp.einsum('ghp,kgpe->kghe', ci, ab_im, precision=hp))
    lag = jnp.arange(t)[None, :] - jnp.arange(t)[:, None]
    toep = jnp.where((lag >= 0)[:, :, None, None, None],
                     kern[jnp.clip(lag, 0, t)], 0.0)
    eye_t = jnp.eye(t, dtype=F32)
    eye_h = jnp.eye(S5_GROUP, dtype=F32)
    toep = toep + (eye_t[:, :, None, None, None] * eye_h[None, None, None]
                   * d_skip.astype(F32)[None, None, :, :, None])
    g_n = toep.shape[2]
    nb, gb = g_n // S5_BLOCK_GROUPS, S5_BLOCK_GROUPS
    eye_g = jnp.eye(gb, dtype=F32)
    width = t * gb * S5_GROUP
    toep = jnp.transpose(toep, (2, 0, 4, 1, 3)).reshape(nb, gb, t, S5_GROUP, t, S5_GROUP)
    toep = jnp.einsum('bgjaih,gk->bjgaikh', toep, eye_g).reshape(nb, width, width)
    rev = t - 1 - jnp.arange(t)

    def state_in(ab):
        x = jnp.transpose(ab[rev], (1, 0, 3, 2)).reshape(nb, gb, t, S5_GROUP, S5_STATE)
        return jnp.einsum('bgjap,gk->bjgakp', x, eye_g).reshape(nb, width, gb * S5_STATE)

    sb = jnp.concatenate([state_in(ab_re), state_in(ab_im)], axis=2)
    e_re, e_im = pw_re[1:], pw_im[1:]
    ca_re = cr[None] * e_re[:, :, None, :] - ci[None] * e_im[:, :, None, :]
    ca_im = -(cr[None] * e_im[:, :, None, :] + ci[None] * e_re[:, :, None, :])

    def state_out(ca):
        x = jnp.transpose(ca, (1, 3, 0, 2)).reshape(nb, gb, S5_STATE, t, S5_GROUP)
        return jnp.einsum('bgpih,gk->bgpikh', x, eye_g).reshape(nb, gb * S5_STATE, width)

    ca = jnp.concatenate([state_out(ca_re), state_out(ca_im)], axis=1)
    at = jnp.stack([pw_re[t].reshape(nb, gb * S5_STATE), pw_im[t].reshape(nb, gb * S5_STATE)], axis=1)
    return [*_split_bf16(toep), *_split_bf16(sb), at, *_split_bf16(ca)]


def _s5_layer(u, ops):
    bsz, seq, ch = u.shape
    t = S5_T
    n_chunks = seq // t
    width = t * LANES
    n_state = 2 * S5_BLOCK_GROUPS * S5_STATE
    row_chunk = next(n_chunks // n for n in (8, 4, 2, 1) if n_chunks % (8 * n) == 0)
    op_spec = lambda r, c: pl.BlockSpec((None, r, c), lambda p, b: (p, 0, 0),
                                        pipeline_mode=pl.Buffered(1))
    slab = pl.BlockSpec((None, seq, LANES), lambda p, b: (b, 0, p))
    return pl.pallas_call(
        functools.partial(_s5_kernel, t=t, row_chunk=row_chunk),
        out_shape=jax.ShapeDtypeStruct((bsz, seq, ch), F32),
        grid=(ch // LANES, bsz),
        in_specs=[slab, op_spec(width, width), op_spec(width, width),
                  op_spec(width, n_state), op_spec(width, n_state), op_spec(2, n_state // 2),
                  op_spec(n_state, width), op_spec(n_state, width)],
        out_specs=slab,
        scratch_shapes=[pltpu.VMEM((n_chunks, width), F32), pltpu.VMEM((n_chunks, n_state), F32)],
        compiler_params=_cparams(2),
        name="s5_chunked",
    )(u, *ops)


def _post_kernel(*refs, n_pad, ff_chunk, glu, final):
    h_ref, a_ref, b_ref = refs[:3]
    pos = 3
    if glu:
        wglu_ref, bglu_ref = refs[pos:pos + 2]
        pos += 2
    wo_ref, gf_ref, wgu_ref, wd_ref = refs[pos:pos + 4]
    pos += 4
    if final:
        gl_ref = refs[pos]
        pos += 1
    o_ref, acc_ref = refs[pos:pos + 2]

    tm = h_ref.shape[0]
    half = a_ref.shape[-1]
    d_ff = wd_ref.shape[0]
    row = lax.broadcasted_iota(jnp.int32, (tm, 1), 0) + pl.program_id(1) * tm
    valid = row >= n_pad

    a = a_ref[...]
    if glu:
        y = b_ref[...]
        z = _dot(y.astype(BF16), wglu_ref[...]) + bglu_ref[...]
        b = (y * (1.0 / (1.0 + jnp.exp(-z)))).astype(BF16)
    else:
        b = b_ref[...]
    mix = _dot(a, wo_ref[:half, :]) + _dot(b, wo_ref[half:, :])
    h1 = jnp.where(valid, h_ref[...] + mix, 0.0)
    hn = _rms(h1, gf_ref[...]).astype(BF16)
    acc_ref[...] = jnp.zeros_like(acc_ref)
    for c in range(0, d_ff, ff_chunk):
        gate = _dot(hn, wgu_ref[:, c:c + ff_chunk])
        up = _dot(hn, wgu_ref[:, d_ff + c:d_ff + c + ff_chunk])
        act = (gate * (1.0 / (1.0 + jnp.exp(-gate))) * up).astype(BF16)
        acc_ref[...] += _dot(act, wd_ref[c:c + ff_chunk, :])
    h2 = jnp.where(valid, h1 + acc_ref[...], 0.0)
    if final:
        h2 = _rms(h2, gl_ref[...])
    o_ref[...] = h2


def _post_mixer(h, a, b, w_out, g_ffn, w_gu, w_down, glu=None, final_g=None):
    bsz, seq, d = h.shape
    tm = next(t for t in (768, 512, 256) if seq % t == 0)
    half = a.shape[-1]
    d_ff = w_down.shape[0]
    row_spec = lambda w: pl.BlockSpec((None, tm, w), lambda bi, i: (bi, i, 0))
    const = lambda shape: _resident(shape, lambda bi, i: (0,) * len(shape))
    args = [h, a, b]
    in_specs = [row_spec(d), row_spec(half), row_spec(half)]
    if glu is not None:
        args += list(glu)
        in_specs += [const((half, half)), const((1, half))]
    args += [w_out, g_ffn, w_gu, w_down]
    in_specs += [const((2 * half, d)), const((1, d)), const((d, 2 * d_ff)), const((d_ff, d))]
    if final_g is not None:
        args.append(final_g)
        in_specs.append(const((1, d)))
    return pl.pallas_call(
        functools.partial(_post_kernel, n_pad=N_PAD, ff_chunk=256,
                          glu=glu is not None, final=final_g is not None),
        out_shape=jax.ShapeDtypeStruct((bsz, seq, d), F32),
        grid=(bsz, seq // tm),
        in_specs=in_specs,
        out_specs=row_spec(d),
        scratch_shapes=[pltpu.VMEM((tm, d), F32)],
        compiler_params=_cparams(2),
        name="outproj_ffn",
    )(*args)


def kernel(x, meta_tokens, norm_mix_g, norm_ffn_g, final_norm_g, even_w_in, even_w_out,
           diff_lam_q1, diff_lam_k1, diff_lam_q2, diff_lam_k2, diff_subln_g, odd_w_in,
           odd_w_out, fox_b_f, s5_lam_re, s5_lam_im, s5_log_dt, s5_b_re, s5_b_im, s5_c_re,
           s5_c_im, s5_d, s5_w_glu, s5_b_glu, ffn_w_gate_up, ffn_w_down):
    bsz, n_seq, d = x.shape
    seq = n_seq + FRONT
    half = d // 2
    n_blk = half // ATT_BLOCK
    scale = HEAD_DIM ** -0.5
    fox_heads = half // HEAD_DIM

    pad = jnp.zeros((bsz, N_PAD, d), x.dtype)
    meta = jnp.broadcast_to(meta_tokens[None].astype(x.dtype), (bsz, N_META, d))
    h = jnp.concatenate([pad, meta, x], axis=1)

    w_in = even_w_in[0]
    col = lambda n: w_in[:, n * half:(n + 1) * half]
    w_tok = jnp.concatenate([col(1), col(4)], axis=1).astype(BF16)
    w_tr = jnp.concatenate([col(0) * scale, col(2), col(3) * scale, col(5)], axis=1).T.astype(BF16)
    tok, tr = _norm_proj(h.reshape(bsz * seq, d), norm_mix_g[0][None, :], w_tok, w_tr)
    sb = _attention("sb", tok, tr, bsz, seq, 0, 0, n_blk, n_blk)
    lam_init = 0.8 - 0.6 * math.exp(-0.3 * 0)
    lam = (jnp.exp(jnp.sum(diff_lam_q1[0].astype(F32) * diff_lam_k1[0].astype(F32)))
           - jnp.exp(jnp.sum(diff_lam_q2[0].astype(F32) * diff_lam_k2[0].astype(F32))) + lam_init)
    g_col = jnp.broadcast_to(diff_subln_g[0].astype(F32)[:, None], (2 * HEAD_DIM, ATT_TILE))
    df = _attention("diff", tok, tr, bsz, seq, n_blk, 2 * n_blk, 3 * n_blk, n_blk,
                    (lam.reshape(1, 1).astype(F32), g_col), out_scale=1.0 - lam_init)
    h = _post_mixer(h, sb.reshape(bsz, seq, half), df.reshape(bsz, seq, half),
                    even_w_out[0].astype(BF16), norm_ffn_g[0][None, :],
                    ffn_w_gate_up[0].astype(BF16), ffn_w_down[0].astype(BF16))

    w_in = odd_w_in[0]
    col = lambda n: w_in[:, n * half:(n + 1) * half]
    w_tok = col(1).astype(BF16)
    w_tr = jnp.concatenate([col(0) * scale, col(2)], axis=1).T.astype(BF16)
    w_f = w_in[:, 3 * half:3 * half + fox_heads]
    w_u = w_in[:, 3 * half + fox_heads:].astype(BF16)
    f_rows = BF16_ROWS
    w_ft = jnp.zeros((f_rows, d), F32).at[:fox_heads].set(w_f.T)
    tok, tr, u, ft = _norm_proj(h.reshape(bsz * seq, d), norm_mix_g[1][None, :], w_tok, w_tr,
                                odd_weights=(w_u,) + _split_bf16(w_ft))
    ft = jnp.transpose(ft.reshape(f_rows, bsz, seq), (1, 0, 2))
    b_f = jnp.zeros((f_rows, 1), F32).at[:fox_heads, 0].set(fox_b_f[0].astype(F32))
    fox = _attention("fox", tok, tr, bsz, seq, 0, 0, n_blk, n_blk,
                     _forget_bias_operands(ft, b_f, fox_heads, n_blk))
    ops = _s5_operators(s5_lam_re[0], s5_lam_im[0], s5_log_dt[0], s5_b_re[0], s5_b_im[0],
                        s5_c_re[0], s5_c_im[0], s5_d[0])
    ssm = _s5_layer(u.reshape(bsz, seq, -1), ops)
    h = _post_mixer(h, fox.reshape(bsz, seq, half), ssm, odd_w_out[0].astype(BF16),
                    norm_ffn_g[1][None, :], ffn_w_gate_up[1].astype(BF16),
                    ffn_w_down[1].astype(BF16),
                    glu=(s5_w_glu[0].astype(BF16), s5_b_glu[0][None, :].astype(F32)),
                    final_g=final_norm_g[None, :])
    return h[:, FRONT:]
```

```python
import functools
import math

import jax
import jax.numpy as jnp
from jax import lax
from jax.experimental import pallas as pl
from jax.experimental.pallas import tpu as pltpu

F32 = jnp.float32
BF16 = jnp.bfloat16

HEAD_DIM = 64
LANES = 128
ATT_BLOCK = 256
ATT_HEADS = ATT_BLOCK // HEAD_DIM
BF16_ROWS = 16
N_META = 16
ATT_TILE = 256
ATT_KTILE = 768
SB_BLOCK = 128
FRONT = ATT_TILE
N_PAD = FRONT - N_META
NEG = -1e30
RMS_EPS = 1e-6
S5_GROUP = 16
S5_STATE = 64
S5_T = 8
S5_BLOCK_GROUPS = LANES // S5_GROUP
VMEM_LIMIT = 56 * 1024 * 1024


def _cparams(n_axes):
    return pltpu.CompilerParams(
        dimension_semantics=("arbitrary",) * n_axes, vmem_limit_bytes=VMEM_LIMIT)


def _resident(shape, index_map):
    return pl.BlockSpec(shape, index_map, pipeline_mode=pl.Buffered(1))


def _trunc_bf16(x):
    bits = lax.bitcast_convert_type(x, jnp.uint32) & jnp.uint32(0xFFFF0000)
    return lax.bitcast_convert_type(bits, F32)


def _split_bf16(w):
    hi = _trunc_bf16(w)
    return hi.astype(BF16), (w - hi).astype(BF16)


def _dot(a, b):
    return jnp.dot(a, b, preferred_element_type=F32)


def _dot_nt(a, b):
    return lax.dot_general(a, b, (((1,), (1,)), ((), ())), preferred_element_type=F32)


def _dot3(a_hi, a_lo, b_hi, b_lo):
    return _dot(a_hi, b_hi) + _dot(a_lo, b_hi) + _dot(a_hi, b_lo)


def _rms(x, g):
    ms = jnp.mean(x * x, axis=-1, keepdims=True)
    return x * lax.rsqrt(ms + RMS_EPS) * g


def _proj_kernel(*refs, n_chunk, odd):
    if odd:
        (x_ref, g_ref, wtok_ref, wtr_ref, wu_ref, wfh_ref, wfl_ref,
         tok_ref, tr_ref, u_ref, ft_ref) = refs
    else:
        x_ref, g_ref, wtok_ref, wtr_ref, tok_ref, tr_ref = refs
    hn32 = _rms(x_ref[...], g_ref[...])
    hn = hn32.astype(BF16)
    for c in range(0, tok_ref.shape[-1], n_chunk):
        tok_ref[:, c:c + n_chunk] = _dot(hn, wtok_ref[:, c:c + n_chunk]).astype(tok_ref.dtype)
    for c in range(0, tr_ref.shape[0], n_chunk):
        tr_ref[c:c + n_chunk, :] = _dot_nt(wtr_ref[c:c + n_chunk, :], hn).astype(tr_ref.dtype)
    if odd:
        u_ref[...] = _dot(hn, wu_ref[...])
        hn_lo = (hn32 - hn.astype(F32)).astype(BF16)
        wfh = wfh_ref[...]
        ft_ref[...] = _dot_nt(wfh, hn) + _dot_nt(wfh, hn_lo) + _dot_nt(wfl_ref[...], hn)


def _norm_proj(h, g, w_tok, w_tr, odd_weights=None):
    m, d = h.shape
    tm = next(t for t in (512, 256) if m % t == 0)
    n_tok, n_tr = w_tok.shape[1], w_tr.shape[0]
    const = lambda shape: _resident(shape, lambda i: (0, 0))
    args = [h, g, w_tok, w_tr]
    in_specs = [pl.BlockSpec((tm, d), lambda i: (i, 0)), const((1, d)),
                const((d, n_tok)), const((n_tr, d))]
    out_shape = [jax.ShapeDtypeStruct((m, n_tok), BF16), jax.ShapeDtypeStruct((n_tr, m), BF16)]
    out_specs = [pl.BlockSpec((tm, n_tok), lambda i: (i, 0)),
                 pl.BlockSpec((n_tr, tm), lambda i: (0, i))]
    if odd_weights is not None:
        wu, wfh, wfl = odd_weights
        nu, nf = wu.shape[1], wfh.shape[0]
        args += [wu, wfh, wfl]
        in_specs += [const((d, nu)), const((nf, d)), const((nf, d))]
        out_shape += [jax.ShapeDtypeStruct((m, nu), F32), jax.ShapeDtypeStruct((nf, m), F32)]
        out_specs += [pl.BlockSpec((tm, nu), lambda i: (i, 0)),
                      pl.BlockSpec((nf, tm), lambda i: (0, i))]
    return pl.pallas_call(
        functools.partial(_proj_kernel, n_chunk=256, odd=odd_weights is not None),
        out_shape=tuple(out_shape),
        grid=(m // tm,),
        in_specs=in_specs,
        out_specs=tuple(out_specs),
        compiler_params=_cparams(1),
        name="norm_proj_odd" if odd_weights is not None else "norm_proj_even",
    )(*args)


def _head_rows(h):
    return slice(h * HEAD_DIM, (h + 1) * HEAD_DIM)


def _head_queries(qt):
    zero = jnp.zeros((HEAD_DIM, qt.shape[1]), qt.dtype)
    return [jnp.concatenate([qt[_head_rows(r)] if r == h else zero for r in range(ATT_HEADS)],
                            axis=0) for h in range(ATT_HEADS)]


def _tile_mask(kind, start, off, tq, tk, n_pad, strict):
    if kind == "mid":
        return None
    row = lax.broadcasted_iota(jnp.int32, (tk, 1), 0)
    if kind == "first":
        return row >= n_pad
    dist = (lax.broadcasted_iota(jnp.int32, (tk, tq), 0)
            - lax.broadcasted_iota(jnp.int32, (tk, tq), 1))
    causal = (dist < off) if strict else (dist <= off)
    return causal & (row + start >= n_pad)


def _for_diag_extent(off, tq, tk, fn, cond=True):
    for rows in range(tq, tk + 1, tq):
        @pl.when(cond & (off == rows - tq))
        def _():
            fn(rows)


def _pipelined_tiles(n, off, tq, tk, produce, consume):
    produce(0, 0)

    @pl.when(n > 0)
    def _():
        produce(1, 1)
        consume(0, 0, "first", tk)

    def body(k, carry):
        for parity in range(2):
            @pl.when(k % 2 == parity)
            def _():
                produce(1 - parity, k + 1)
                consume(parity, k, "mid", tk)
        return carry

    lax.fori_loop(1, n, body, 0)
    for parity in range(2):
        _for_diag_extent(off, tq, tk, lambda rows: consume(parity, n, "diag", rows),
                         cond=n % 2 == parity)


def _sb_kernel(k_ref, qt_ref, vt_ref, tri_ref, o_ref, acc_ref, carry_ref, *, tq, tk, blk, n_pad):
    i = pl.program_id(2)
    ratio = tk // tq
    c_diag = i // ratio
    off = (i - c_diag * ratio) * tq
    q_cat = jnp.concatenate(_head_queries(qt_ref[...]), axis=1)
    tri = tri_ref[...]
    acc_ref[...] = jnp.zeros_like(acc_ref)
    carry_ref[...] = jnp.zeros_like(carry_ref)

    def step(c, kind, n_keys=tk):
        start = pl.multiple_of(c * tk, tk)
        ks = k_ref[pl.ds(start, n_keys), :]
        vt = vt_ref[:, pl.ds(start, n_keys)]
        mask = _tile_mask(kind, start, off, tq, n_keys, n_pad, strict=True)
        logits = _dot(ks, q_cat)
        n_blocks = n_keys // blk
        for h in range(ATT_HEADS):
            x = logits[:, h * tq:(h + 1) * tq]
            sp = jnp.maximum(x, 0.0) + jnp.log(1.0 + jnp.exp(-jnp.abs(x)))
            log_beta = x - sp
            if mask is not None:
                sp = jnp.where(mask, sp, 0.0)
            hi = sp.astype(BF16)
            lo = (sp - hi.astype(F32)).astype(BF16)
            blocks = [slice(b * blk, (b + 1) * blk) for b in range(n_blocks)]
            within = _dot(tri, jnp.concatenate(
                [jnp.concatenate([hi[r], lo[r]], axis=0) for r in blocks], axis=1))
            run = carry_ref[h]
            w_blocks = [None] * n_blocks
            for b in reversed(range(n_blocks)):
                r = blocks[b]
                w_blocks[b] = jnp.exp(log_beta[r] + within[:, b * tq:(b + 1) * tq] + run)
                run = run - jnp.sum(sp[r], axis=0, keepdims=True)
            carry_ref[h] = run
            w = jnp.concatenate(w_blocks, axis=0)
            if mask is not None:
                w = jnp.where(mask, w, 0.0)
            rows = _head_rows(h)
            acc_ref[rows, :] += _dot(vt[rows, :], w.astype(BF16))

    _for_diag_extent(off, tq, tk, lambda n_keys: step(c_diag, "diag", n_keys))

    def mid(n, carry):
        step(c_diag - n, "mid")
        return carry

    lax.fori_loop(1, c_diag, mid, 0)

    @pl.when(c_diag > 0)
    def _():
        step(0, "first")

    o_ref[...] = acc_ref[...].T.astype(o_ref.dtype)


def _fox_queries(qt, qb):
    tq = qt.shape[1]
    zeros = lambda n: jnp.zeros((n, tq), qt.dtype)
    out = []
    for h in range(ATT_HEADS):
        own = [qt[_head_rows(h)], zeros(HEAD_DIM)]
        pieces = (own if h % 2 == 0 else own[::-1]) + [
            zeros(h * BF16_ROWS), qb[h * BF16_ROWS:(h + 1) * BF16_ROWS],
            zeros(2 * HEAD_DIM - (h + 1) * BF16_ROWS)]
        out.append(jnp.concatenate([p for p in pieces if p.shape[0]], axis=0))
    return out


def _softmax_attn_kernel(*refs, tq, tk, n_pad, mode, out_scale):
    if mode == "fox":
        k_ref, kb_ref, qt_ref, qb_ref, vt_ref, o_ref, acc_ref, m_ref, sc_ref = refs
        q_heads = _fox_queries(qt_ref[...], qb_ref[...])
        q_cats = [jnp.concatenate(q_heads[2 * p:2 * p + 2], axis=1) for p in range(ATT_HEADS // 2)]
    else:
        k_ref, qt_ref, vt_ref, lam_ref, g_ref, o_ref, acc_ref, m_ref, sc_ref = refs
        q_cats = [jnp.concatenate(_head_queries(qt_ref[...]), axis=1)]
    i = pl.program_id(2)
    ratio = tk // tq
    c_diag = i // ratio
    off = (i - c_diag * ratio) * tq
    v_rows = HEAD_DIM if mode == "fox" else 2 * HEAD_DIM
    acc_ref[...] = jnp.zeros_like(acc_ref)
    m_ref[...] = jnp.full_like(m_ref, NEG)

    def produce(slot, c):
        rows = pl.ds(pl.multiple_of(c * tk, tk), tk)
        ks = k_ref[rows, :]
        if mode == "fox":
            kb = kb_ref[rows, :]
            k_ops = [jnp.concatenate([ks[:, p * LANES:(p + 1) * LANES], kb], axis=1)
                     for p in range(ATT_HEADS // 2)]
        else:
            k_ops = [ks]
        width = ATT_HEADS * tq // len(q_cats)
        for n, (k_op, q) in enumerate(zip(k_ops, q_cats)):
            sc_ref[slot, :, n * width:(n + 1) * width] = _dot(k_op, q)

    def consume(slot, c, kind, rows):
        start = pl.multiple_of(c * tk, tk)
        vt = vt_ref[:, pl.ds(start, rows)]
        mask = _tile_mask(kind, start, off, tq, rows, n_pad, strict=False)
        ones_rows = (lax.broadcasted_iota(jnp.int32, (BF16_ROWS, rows), 0) == 0).astype(BF16)
        for h in range(ATT_HEADS):
            s = sc_ref[slot, :rows, h * tq:(h + 1) * tq]
            if mask is not None:
                s = jnp.where(mask, s, NEG)
            m_old = m_ref[h]
            m_new = jnp.maximum(m_old, jnp.max(s, axis=0, keepdims=True))
            alpha = jnp.exp(m_old - m_new)
            p = jnp.exp(s - m_new).astype(BF16)
            m_ref[h] = m_new
            vh = vt[_head_rows(h)] if mode == "fox" else vt[(h // 2) * v_rows:(h // 2 + 1) * v_rows]
            v_aug = jnp.concatenate([vh, ones_rows], axis=0)
            acc_ref[h] = alpha * acc_ref[h] + _dot(v_aug, p)

    _pipelined_tiles(c_diag, off, tq, tk, produce, consume)

    outs = []
    for h in range(ATT_HEADS):
        acc = acc_ref[h]
        outs.append(acc[:v_rows] * (1.0 / acc[v_rows:v_rows + 1]))
    if mode == "diff":
        heads = []
        for d in range(ATT_HEADS // 2):
            out_d = outs[2 * d] - lam_ref[0, 0] * outs[2 * d + 1]
            ms = jnp.mean(out_d * out_d, axis=0, keepdims=True)
            heads.append(out_d * lax.rsqrt(ms + RMS_EPS) * g_ref[...] * out_scale)
        outs = heads
    o_ref[...] = jnp.concatenate(outs, axis=0).T.astype(o_ref.dtype)


def _attention(mode, tok, tr, bsz, seq, k_blk, q_blk, v_blk, n_blk, extra=(), out_scale=1.0):
    tq, tk = ATT_TILE, ATT_KTILE
    nq = seq // tq
    k_spec = pl.BlockSpec((seq, ATT_BLOCK), lambda b, p, i: (b, k_blk + p))
    qt_spec = pl.BlockSpec((ATT_BLOCK, tq), lambda b, p, i: (q_blk + p, b * nq + i))
    vt_spec = pl.BlockSpec((ATT_BLOCK, seq), lambda b, p, i: (v_blk + p, b))
    m_scratch = pltpu.VMEM((ATT_HEADS, 1, tq), F32)
    sc_scratch = pltpu.VMEM((2, tk, ATT_HEADS * tq), F32)
    if mode == "sb":
        idx = jnp.arange(SB_BLOCK)
        tri = -(idx[None, :] > idx[:, None]).astype(BF16)
        args = [tok, tr, tr, jnp.concatenate([tri, tri], axis=1)]
        in_specs = [k_spec, qt_spec, vt_spec,
                    _resident((SB_BLOCK, 2 * SB_BLOCK), lambda b, p, i: (0, 0))]
        scratch = [pltpu.VMEM((ATT_BLOCK, tq), F32), m_scratch]
        body = functools.partial(_sb_kernel, tq=tq, tk=tk, blk=SB_BLOCK, n_pad=N_PAD)
    elif mode == "fox":
        k_bias, q_bias = extra
        args = [tok, k_bias, tr, q_bias, tr]
        in_specs = [k_spec, pl.BlockSpec((seq, LANES), lambda b, p, i: (b, p)), qt_spec,
                    pl.BlockSpec((None, None, ATT_HEADS * BF16_ROWS, tq), lambda b, p, i: (b, p, 0, i)),
                    vt_spec]
        scratch = [pltpu.VMEM((ATT_HEADS, HEAD_DIM + BF16_ROWS, tq), F32), m_scratch, sc_scratch]
        body = functools.partial(_softmax_attn_kernel, tq=tq, tk=tk, n_pad=N_PAD, mode=mode,
                                 out_scale=out_scale)
    else:
        lam, g_col = extra
        args = [tok, tr, tr, lam, g_col]
        in_specs = [k_spec, qt_spec, vt_spec, pl.BlockSpec(memory_space=pltpu.SMEM),
                    _resident((2 * HEAD_DIM, tq), lambda b, p, i: (0, 0))]
        scratch = [pltpu.VMEM((ATT_HEADS, 2 * HEAD_DIM + BF16_ROWS, tq), F32), m_scratch, sc_scratch]
        body = functools.partial(_softmax_attn_kernel, tq=tq, tk=tk, n_pad=N_PAD, mode=mode,
                                 out_scale=out_scale)
    return pl.pallas_call(
        body,
        out_shape=jax.ShapeDtypeStruct((bsz * seq, n_blk * ATT_BLOCK), BF16),
        grid=(bsz, n_blk, nq),
        in_specs=in_specs,
        out_specs=pl.BlockSpec((tq, ATT_BLOCK), lambda b, p, i: (b * nq + i, p)),
        scratch_shapes=scratch,
        compiler_params=_cparams(3),
        name=mode + "_attention",
    )(*args)


def _split3_bf16(x):
    hi = x.astype(BF16)
    r1 = x - hi.astype(F32)
    mid = r1.astype(BF16)
    lo = (r1 - mid.astype(F32)).astype(BF16)
    return hi, mid, lo


def _fcum_kernel(ft_ref, b_ref, tri_ref, selq_ref, selk_ref, qb_ref, kb_ref, *, tile, n_pad):
    rows, seq = ft_ref.shape
    tri = tri_ref[...]
    one_row = (lax.broadcasted_iota(jnp.int32, (rows, tile), 0) == 0).astype(BF16)
    carry = jnp.zeros((rows, 1), F32)
    for c in range(seq // tile):
        cols = slice(c * tile, (c + 1) * tile)
        x = ft_ref[:, cols] + b_ref[...]
        log_f = jnp.minimum(x, 0.0) - jnp.log(1.0 + jnp.exp(-jnp.abs(x)))
        pos = lax.broadcasted_iota(jnp.int32, (1, tile), 1) + c * tile
        log_f = jnp.where(pos >= n_pad, log_f, 0.0)
        f_cum = _dot(jnp.concatenate(_split3_bf16(log_f), axis=1), tri) + carry
        carry = carry + jnp.sum(log_f, axis=1, keepdims=True)
        terms = jnp.concatenate(_split3_bf16(f_cum) + (one_row,), axis=0).astype(F32)
        qb_ref[:, cols] = _dot(selq_ref[...], terms).astype(qb_ref.dtype)
        kb_ref[cols, :] = _dot(selk_ref[...], terms).T.astype(kb_ref.dtype)


def _forget_bias_operands(ft, b_f, heads, n_blk):
    bsz, rows, seq = ft.shape
    tile = ATT_TILE
    idx = jnp.arange(tile)
    tri = (idx[:, None] <= idx[None, :]).astype(BF16)
    tri3 = jnp.concatenate([tri, tri, tri], axis=0)
    h = jnp.arange(heads)
    q_row = h * BF16_ROWS
    k_lane = (h // ATT_HEADS) * LANES + (h % ATT_HEADS) * BF16_ROWS
    selq = jnp.zeros((heads * BF16_ROWS, 4 * rows), F32)
    selk = jnp.zeros((n_blk * LANES, 4 * rows), F32)
    for term in range(3):
        selq = selq.at[q_row + 3 + term, term * rows + h].set(1.0)
        selq = selq.at[q_row + term, 3 * rows].set(-1.0)
        selk = selk.at[k_lane + term, term * rows + h].set(1.0)
        selk = selk.at[k_lane + 3 + term, 3 * rows].set(1.0)
    const = lambda shape: _resident(shape, lambda b: (0, 0))
    q_bias, k_bias = pl.pallas_call(
        functools.partial(_fcum_kernel, tile=tile, n_pad=N_PAD),
        out_shape=(jax.ShapeDtypeStruct((bsz, heads * BF16_ROWS, seq), BF16),
                   jax.ShapeDtypeStruct((bsz, seq, n_blk * LANES), BF16)),
        grid=(bsz,),
        in_specs=[pl.BlockSpec((None, rows, seq), lambda b: (b, 0, 0)),
                  const((rows, 1)), const((3 * tile, tile)),
                  const(selq.shape), const(selk.shape)],
        out_specs=(pl.BlockSpec((None, heads * BF16_ROWS, seq), lambda b: (b, 0, 0)),
                   pl.BlockSpec((None, seq, n_blk * LANES), lambda b: (b, 0, 0))),
        compiler_params=_cparams(1),
        name="forget_cumsum",
    )(ft, b_f, tri3, selq, selk)
    return (k_bias.reshape(bsz * seq, n_blk * LANES),
            q_bias.reshape(bsz, n_blk, ATT_HEADS * BF16_ROWS, seq))


def _s5_kernel(u_ref, toep_h, toep_l, sb_h, sb_l, at_ref, ca_h, ca_l, y_ref, acc_ref, st_ref,
               *, t, row_chunk):
    n_chunks = st_ref.shape[0]
    half = st_ref.shape[1] // 2

    def split(x):
        hi = x.astype(BF16)
        return hi, (x - hi.astype(F32)).astype(BF16)

    def chunk_rows(ref, r, step):
        return ref.at[pl.ds(r * row_chunk * t + step, row_chunk, stride=t), :]

    def stage1(r, c):
        ut = jnp.concatenate([chunk_rows(u_ref, r, step)[...] for step in range(t)], axis=1)
        uh, ul = split(ut)
        sl = pl.ds(pl.multiple_of(r * row_chunk, 8), row_chunk)
        acc_ref[sl, :] = _dot3(uh, ul, toep_h[...], toep_l[...])
        st_ref[sl, :] = _dot3(uh, ul, sb_h[...], sb_l[...])
        return c

    lax.fori_loop(0, n_chunks // row_chunk, stage1, 0)

    a_re = at_ref[0:1, :]
    a_im = at_ref[1:2, :]

    def scan(c, state):
        x_re, x_im = state
        row = pl.ds(c, 1)
        s_re = st_ref[row, :half]
        s_im = st_ref[row, half:]
        st_ref[row, :half] = x_re
        st_ref[row, half:] = x_im
        return (a_re * x_re - a_im * x_im + s_re, a_re * x_im + a_im * x_re + s_im)

    zero = jnp.zeros((1, half), F32)
    lax.fori_loop(0, n_chunks, scan, (zero, zero), unroll=4)

    def stage3(r, c):
        sl = pl.ds(pl.multiple_of(r * row_chunk, 8), row_chunk)
        xh, xl = split(st_ref[sl, :])
        y = acc_ref[sl, :] + _dot3(xh, xl, ca_h[...], ca_l[...])
        y = 0.5 * y * (1.0 + jnp.tanh(math.sqrt(2.0 / math.pi) * (y + 0.044715 * (y * y * y))))
        for step in range(t):
            chunk_rows(y_ref, r, step)[...] = y[:, step * LANES:(step + 1) * LANES]
        return c

    lax.fori_loop(0, n_chunks // row_chunk, stage3, 0)


def _s5_operators(lam_re, lam_im, log_dt, b_re, b_im, c_re, c_im, d_skip):
    t = S5_T
    lre, lim = lam_re.astype(F32), lam_im.astype(F32)
    dt = jnp.exp(log_dt.astype(F32))[:, None]
    mag = jnp.exp(lre * dt)
    a_re, a_im = mag * jnp.cos(lim * dt), mag * jnp.sin(lim * dt)
    den = lre * lre + lim * lim
    g_re = ((a_re - 1.0) * lre + a_im * lim) / den
    g_im = (a_im * lre - (a_re - 1.0) * lim) / den
    br, bi = b_re.astype(F32), b_im.astype(F32)
    bb_re = g_re[..., None] * br - g_im[..., None] * bi
    bb_im = g_re[..., None] * bi + g_im[..., None] * br
    k = jnp.arange(t + 1, dtype=F32)[:, None, None]
    pw_mag = jnp.exp(k * (lre * dt)[None])
    pw_re = pw_mag * jnp.cos(k * (lim * dt)[None])
    pw_im = pw_mag * jnp.sin(k * (lim * dt)[None])
    cr, ci = c_re.astype(F32), c_im.astype(F32)
    hp = lax.Precision.HIGHEST
    ab_re = pw_re[..., None] * bb_re[None] - pw_im[..., None] * bb_im[None]
    ab_im = pw_re[..., None] * bb_im[None] + pw_im[..., None] * bb_re[None]
    kern = (jnp.einsum('ghp,kgpe->kghe', cr, ab_re, precision=hp)
            - jnp.einsum('ghp,kgpe->kghe', ci, ab_im, precision=hp))
    lag = jnp.arange(t)[None, :] - jnp.arange(t)[:, None]
    toep = jnp.where((lag >= 0)[:, :, None, None, None],
                     kern[jnp.clip(lag, 0, t)], 0.0)
    eye_t = jnp.eye(t, dtype=F32)
    eye_h = jnp.eye(S5_GROUP, dtype=F32)
    toep = toep + (eye_t[:, :, None, None, None] * eye_h[None, None, None]
                   * d_skip.astype(F32)[None, None, :, :, None])
    g_n = toep.shape[2]
    nb, gb = g_n // S5_BLOCK_GROUPS, S5_BLOCK_GROUPS
    eye_g = jnp.eye(gb, dtype=F32)
    width = t * gb * S5_GROUP
    toep = jnp.transpose(toep, (2, 0, 4, 1, 3)).reshape(nb, gb, t, S5_GROUP, t, S5_GROUP)
    toep = jnp.einsum('bgjaih,gk->bjgaikh', toep, eye_g).reshape(nb, width, width)
    rev = t - 1 - jnp.arange(t)

    def state_in(ab):
        x = jnp.transpose(ab[rev], (1, 0, 3, 2)).reshape(nb, gb, t, S5_GROUP, S5_STATE)
        return jnp.einsum('bgjap,gk->bjgakp', x, eye_g).reshape(nb, width, gb * S5_STATE)

    sb = jnp.concatenate([state_in(ab_re), state_in(ab_im)], axis=2)
    e_re, e_im = pw_re[1:], pw_im[1:]
    ca_re = cr[None] * e_re[:, :, None, :] - ci[None] * e_im[:, :, None, :]
    ca_im = -(cr[None] * e_im[:, :, None, :] + ci[None] * e_re[:, :, None, :])

    def state_out(ca):
        x = jnp.transpose(ca, (1, 3, 0, 2)).reshape(nb, gb, S5_STATE, t, S5_GROUP)
        return jnp.einsum('bgpih,gk->bgpikh', x, eye_g).reshape(nb, gb * S5_STATE, width)

    ca = jnp.concatenate([state_out(ca_re), state_out(ca_im)], axis=1)
    at = jnp.stack([pw_re[t].reshape(nb, gb * S5_STATE), pw_im[t].reshape(nb, gb * S5_STATE)], axis=1)
    return [*_split_bf16(toep), *_split_bf16(sb), at, *_split_bf16(ca)]


def _s5_layer(u, ops):
    bsz, seq, ch = u.shape
    t = S5_T
    n_chunks = seq // t
    width = t * LANES
    n_state = 2 * S5_BLOCK_GROUPS * S5_STATE
    row_chunk = next(n_chunks // n for n in (8, 4, 2, 1) if n_chunks % (8 * n) == 0)
    op_spec = lambda r, c: pl.BlockSpec((None, r, c), lambda p, b: (p, 0, 0),
                                        pipeline_mode=pl.Buffered(1))
    slab = pl.BlockSpec((None, seq, LANES), lambda p, b: (b, 0, p))
    return pl.pallas_call(
        functools.partial(_s5_kernel, t=t, row_chunk=row_chunk),
        out_shape=jax.ShapeDtypeStruct((bsz, seq, ch), F32),
        grid=(ch // LANES, bsz),
        in_specs=[slab, op_spec(width, width), op_spec(width, width),
                  op_spec(width, n_state), op_spec(width, n_state), op_spec(2, n_state // 2),
                  op_spec(n_state, width), op_spec(n_state, width)],
        out_specs=slab,
        scratch_shapes=[pltpu.VMEM((n_chunks, width), F32), pltpu.VMEM((n_chunks, n_state), F32)],
        compiler_params=_cparams(2),
        name="s5_chunked",
    )(u, *ops)


def _post_kernel(*refs, n_pad, ff_chunk, glu, final):
    h_ref, a_ref, b_ref = refs[:3]
    pos = 3
    if glu:
        wglu_ref, bglu_ref = refs[pos:pos + 2]
        pos += 2
    wo_ref, gf_ref, wgu_ref, wd_ref = refs[pos:pos + 4]
    pos += 4
    if final:
        gl_ref = refs[pos]
        pos += 1
    o_ref, acc_ref = refs[pos:pos + 2]

    tm = h_ref.shape[0]
    half = a_ref.shape[-1]
    d_ff = wd_ref.shape[0]
    row = lax.broadcasted_iota(jnp.int32, (tm, 1), 0) + pl.program_id(1) * tm
    valid = row >= n_pad

    a = a_ref[...]
    if glu:
        y = b_ref[...]
        z = _dot(y.astype(BF16), wglu_ref[...]) + bglu_ref[...]
        b = (y * (1.0 / (1.0 + jnp.exp(-z)))).astype(BF16)
    else:
        b = b_ref[...]
    mix = _dot(a, wo_ref[:half, :]) + _dot(b, wo_ref[half:, :])
    h1 = jnp.where(valid, h_ref[...] + mix, 0.0)
    hn = _rms(h1, gf_ref[...]).astype(BF16)
    acc_ref[...] = jnp.zeros_like(acc_ref)
    for c in range(0, d_ff, ff_chunk):
        gate = _dot(hn, wgu_ref[:, c:c + ff_chunk])
        up = _dot(hn, wgu_ref[:, d_ff + c:d_ff + c + ff_chunk])
        act = (gate * (1.0 / (1.0 + jnp.exp(-gate))) * up).astype(BF16)
        acc_ref[...] += _dot(act, wd_ref[c:c + ff_chunk, :])
    h2 = jnp.where(valid, h1 + acc_ref[...], 0.0)
    if final:
        h2 = _rms(h2, gl_ref[...])
    o_ref[...] = h2


def _post_mixer(h, a, b, w_out, g_ffn, w_gu, w_down, glu=None, final_g=None):
    bsz, seq, d = h.shape
    tm = next(t for t in (768, 512, 256) if seq % t == 0)
    half = a.shape[-1]
    d_ff = w_down.shape[0]
    row_spec = lambda w: pl.BlockSpec((None, tm, w), lambda bi, i: (bi, i, 0))
    const = lambda shape: _resident(shape, lambda bi, i: (0,) * len(shape))
    args = [h, a, b]
    in_specs = [row_spec(d), row_spec(half), row_spec(half)]
    if glu is not None:
        args += list(glu)
        in_specs += [const((half, half)), const((1, half))]
    args += [w_out, g_ffn, w_gu, w_down]
    in_specs += [const((2 * half, d)), const((1, d)), const((d, 2 * d_ff)), const((d_ff, d))]
    if final_g is not None:
        args.append(final_g)
        in_specs.append(const((1, d)))
    return pl.pallas_call(
        functools.partial(_post_kernel, n_pad=N_PAD, ff_chunk=256,
                          glu=glu is not None, final=final_g is not None),
        out_shape=jax.ShapeDtypeStruct((bsz, seq, d), F32),
        grid=(bsz, seq // tm),
        in_specs=in_specs,
        out_specs=row_spec(d),
        scratch_shapes=[pltpu.VMEM((tm, d), F32)],
        compiler_params=_cparams(2),
        name="outproj_ffn",
    )(*args)


def kernel(x, meta_tokens, norm_mix_g, norm_ffn_g, final_norm_g, even_w_in, even_w_out,
           diff_lam_q1, diff_lam_k1, diff_lam_q2, diff_lam_k2, diff_subln_g, odd_w_in,
           odd_w_out, fox_b_f, s5_lam_re, s5_lam_im, s5_log_dt, s5_b_re, s5_b_im, s5_c_re,
           s5_c_im, s5_d, s5_w_glu, s5_b_glu, ffn_w_gate_up, ffn_w_down):
    bsz, n_seq, d = x.shape
    seq = n_seq + FRONT
    half = d // 2
    n_blk = half // ATT_BLOCK
    scale = HEAD_DIM ** -0.5
    fox_heads = half // HEAD_DIM

    pad = jnp.zeros((bsz, N_PAD, d), x.dtype)
    meta = jnp.broadcast_to(meta_tokens[None].astype(x.dtype), (bsz, N_META, d))
    h = jnp.concatenate([pad, meta, x], axis=1)

    w_in = even_w_in[0]
    col = lambda n: w_in[:, n * half:(n + 1) * half]
    w_tok = jnp.concatenate([col(1), col(4)], axis=1).astype(BF16)
    w_tr = jnp.concatenate([col(0) * scale, col(2), col(3) * scale, col(5)], axis=1).T.astype(BF16)
    tok, tr = _norm_proj(h.reshape(bsz * seq, d), norm_mix_g[0][None, :], w_tok, w_tr)
    sb = _attention("sb", tok, tr, bsz, seq, 0, 0, n_blk, n_blk)
    lam_init = 0.8 - 0.6 * math.exp(-0.3 * 0)
    lam = (jnp.exp(jnp.sum(diff_lam_q1[0].astype(F32) * diff_lam_k1[0].astype(F32)))
           - jnp.exp(jnp.sum(diff_lam_q2[0].astype(F32) * diff_lam_k2[0].astype(F32))) + lam_init)
    g_col = jnp.broadcast_to(diff_subln_g[0].astype(F32)[:, None], (2 * HEAD_DIM, ATT_TILE))
    df = _attention("diff", tok, tr, bsz, seq, n_blk, 2 * n_blk, 3 * n_blk, n_blk,
                    (lam.reshape(1, 1).astype(F32), g_col), out_scale=1.0 - lam_init)
    h = _post_mixer(h, sb.reshape(bsz, seq, half), df.reshape(bsz, seq, half),
                    even_w_out[0].astype(BF16), norm_ffn_g[0][None, :],
                    ffn_w_gate_up[0].astype(BF16), ffn_w_down[0].astype(BF16))

    w_in = odd_w_in[0]
    col = lambda n: w_in[:, n * half:(n + 1) * half]
    w_tok = col(1).astype(BF16)
    w_tr = jnp.concatenate([col(0) * scale, col(2)], axis=1).T.astype(BF16)
    w_f = w_in[:, 3 * half:3 * half + fox_heads]
    w_u = w_in[:, 3 * half + fox_heads:].astype(BF16)
    f_rows = BF16_ROWS
    w_ft = jnp.zeros((f_rows, d), F32).at[:fox_heads].set(w_f.T)
    tok, tr, u, ft = _norm_proj(h.reshape(bsz * seq, d), norm_mix_g[1][None, :], w_tok, w_tr,
                                odd_weights=(w_u,) + _split_bf16(w_ft))
    ft = jnp.transpose(ft.reshape(f_rows, bsz, seq), (1, 0, 2))
    b_f = jnp.zeros((f_rows, 1), F32).at[:fox_heads, 0].set(fox_b_f[0].astype(F32))
    fox = _attention("fox", tok, tr, bsz, seq, 0, 0, n_blk, n_blk,
                     _forget_bias_operands(ft, b_f, fox_heads, n_blk))
    ops = _s5_operators(s5_lam_re[0], s5_lam_im[0], s5_log_dt[0], s5_b_re[0], s5_b_im[0],
                        s5_c_re[0], s5_c_im[0], s5_d[0])
    ssm = _s5_layer(u.reshape(bsz, seq, -1), ops)
    h = _post_mixer(h, fox.reshape(bsz, seq, half), ssm, odd_w_out[0].astype(BF16),
                    norm_ffn_g[1][None, :], ffn_w_gate_up[1].astype(BF16),
                    ffn_w_down[1].astype(BF16),
                    glu=(s5_w_glu[0].astype(BF16), s5_b_glu[0][None, :].astype(F32)),
                    final_g=final_norm_g[None, :])
    return h[:, FRONT:]
```
